```python
import jax, jax.numpy as jnp
from jax import lax
import numpy as np

D_MODEL = 1024
BATCH = 4
SEQ = 8192
DEPTH = 4

N_MIXERS = 2
N_RET_LAYERS = (DEPTH + 1) // 2
N_MLA_LAYERS = DEPTH // 2

ROPE_BASE = 10000.0
NORM_EPS = 1e-6

RET_HEADS = 4
RET_QK_DIM = D_MODEL // RET_HEADS
RET_V_DIM = 2 * RET_QK_DIM
RET_WIDTH = RET_HEADS * RET_V_DIM
RET_CHUNK = 128

MLA_HEADS = 16
MLA_NOPE = 128
MLA_ROPE = 64
MLA_V = 128
Q_LORA = 3 * D_MODEL // 4
KV_LORA = D_MODEL // 4
MLA_WIDTH = MLA_HEADS * MLA_V
Q_BLOCK = 128

PLE_DIM = 256

kernel_name = "hybrid_retention_mla_sandwich_ple"


def rms_norm(x, g):
    xf = x.astype(jnp.float32)
    y = xf * lax.rsqrt(jnp.mean(xf * xf, axis=-1, keepdims=True) + NORM_EPS)
    return (y * g.astype(jnp.float32)).astype(x.dtype)


def rope(x, pos):
    half = x.shape[-1] // 2
    inv_freq = ROPE_BASE ** (-jnp.arange(half, dtype=jnp.float32) / half)
    ang = pos.astype(jnp.float32)[..., None] * inv_freq
    cos = jnp.cos(ang)[:, :, None, :]
    sin = jnp.sin(ang)[:, :, None, :]
    xf = x.astype(jnp.float32)
    x1, x2 = xf[..., :half], xf[..., half:]
    return jnp.concatenate([x1 * cos - x2 * sin, x2 * cos + x1 * sin], axis=-1).astype(x.dtype)


def retention_branch(h, pos, w_in, gn_g, w_out):
    B, S, _ = h.shape
    n_chunks = S // RET_CHUNK
    z = h @ w_in
    qk_w = RET_HEADS * RET_QK_DIM
    q = z[..., :qk_w].reshape(B, S, RET_HEADS, RET_QK_DIM)
    k = z[..., qk_w:2 * qk_w].reshape(B, S, RET_HEADS, RET_QK_DIM)
    v = z[..., 2 * qk_w:2 * qk_w + RET_WIDTH].reshape(B, S, RET_HEADS, RET_V_DIM)
    gate = z[..., 2 * qk_w + RET_WIDTH:]
    q = rope(q, pos)
    k = rope(k, pos) * (RET_QK_DIM ** -0.5)

    log_g = jnp.log(1.0 - 2.0 ** (-5.0 - jnp.arange(RET_HEADS, dtype=jnp.float32)))
    idx = jnp.arange(RET_CHUNK, dtype=jnp.float32)
    rel = idx[:, None] - idx[None, :]
    dmask = jnp.where(rel[None] >= 0, jnp.exp(jnp.maximum(rel, 0.0)[None] * log_g[:, None, None]), 0.0)
    q_decay = jnp.exp((idx + 1.0)[None, :] * log_g[:, None])
    k_decay = jnp.exp((RET_CHUNK - 1.0 - idx)[None, :] * log_g[:, None])
    c_decay = jnp.exp(RET_CHUNK * log_g)

    def to_chunks(t):
        return t.astype(jnp.float32).reshape(B, n_chunks, RET_CHUNK, RET_HEADS, t.shape[-1]).transpose(1, 0, 3, 2, 4)

    qc, kc, vc = to_chunks(q), to_chunks(k), to_chunks(v)

    def step(state, inp):
        qb, kb, vb = inp
        a = jnp.einsum('bhid,bhjd->bhij', qb, kb) * dmask
        o = jnp.einsum('bhij,bhjv->bhiv', a, vb) + jnp.einsum(
            'bhid,bhdv->bhiv', qb * q_decay[..., None], state)
        state = state * c_decay[:, None, None] + jnp.einsum(
            'bhjd,bhjv->bhdv', kb * k_decay[..., None], vb)
        return state, o

    s0 = jnp.zeros((B, RET_HEADS, RET_QK_DIM, RET_V_DIM), jnp.float32)
    _, o = lax.scan(step, s0, (qc, kc, vc))
    o = o.transpose(1, 0, 3, 2, 4).reshape(B, S, RET_HEADS, RET_V_DIM)
    o = o * lax.rsqrt(jnp.mean(o * o, axis=-1, keepdims=True) + NORM_EPS)
    o = (o.reshape(B, S, RET_WIDTH) * gn_g.astype(jnp.float32)).astype(h.dtype)
    return (jax.nn.silu(gate) * o) @ w_out


def mla_branch(h, pos, w_in, q_norm_g, kv_norm_g, w_uq, w_uk, w_uv, w_out):
    B, S, _ = h.shape
    n_blocks = S // Q_BLOCK
    z = h @ w_in
    o1, o2, o3 = Q_LORA, Q_LORA + KV_LORA, Q_LORA + KV_LORA + MLA_ROPE
    c_q = rms_norm(z[..., :o1], q_norm_g)
    c_kv = rms_norm(z[..., o1:o2], kv_norm_g)
    k_rope = rope(z[..., o2:o3][:, :, None, :], pos)[:, :, 0, :]
    gate = z[..., o3:]

    q = (c_q @ w_uq).reshape(B, S, MLA_HEADS, MLA_NOPE + MLA_ROPE)
    q_nope = q[..., :MLA_NOPE]
    q_rope = rope(q[..., MLA_NOPE:], pos)
    q_abs = jnp.einsum('bshd,chd->bshc', q_nope, w_uk)
    scale = (MLA_NOPE + MLA_ROPE) ** -0.5
    q_cat = jnp.concatenate([q_abs, q_rope], axis=-1) * scale
    k_cat = jnp.concatenate([c_kv, k_rope], axis=-1)
    q_blocks = q_cat.reshape(B, n_blocks, Q_BLOCK, MLA_HEADS, -1).transpose(1, 0, 2, 3, 4)
    key_idx = jnp.arange(S)
    neg = jnp.finfo(jnp.float32).min

    def attend(args):
        qb, bi = args
        s = jnp.einsum('bqhc,bkc->bhqk', qb, k_cat).astype(jnp.float32)
        q_idx = bi * Q_BLOCK + jnp.arange(Q_BLOCK)
        s = jnp.where(key_idx[None, :] <= q_idx[:, None], s, neg)
        pr = jax.nn.softmax(s, axis=-1).astype(c_kv.dtype)
        return jnp.einsum('bhqk,bkc->bqhc', pr, c_kv)

    o_lat = lax.map(attend, (q_blocks, jnp.arange(n_blocks)))
    o_lat = o_lat.transpose(1, 0, 2, 3, 4).reshape(B, S, MLA_HEADS, KV_LORA)
    o = jnp.einsum('bshc,chd->bshd', o_lat, w_uv).reshape(B, S, MLA_WIDTH)
    return (jax.nn.silu(gate) * o) @ w_out


def setup_inputs(seed: int = 0) -> dict:
    key = jax.random.key(seed)
    ks = jax.random.split(key, 20)
    f32 = jnp.float32

    def w(k, shape, fan_in):
        return jax.random.normal(k, shape, f32) * (fan_in ** -0.5)

    def gain(k, shape):
        return 1.0 + 0.02 * jax.random.normal(k, shape, f32)

    ret_in_w = 2 * RET_HEADS * RET_QK_DIM + 2 * RET_WIDTH
    mla_in_w = Q_LORA + KV_LORA + MLA_ROPE + MLA_WIDTH
    offset = jax.random.randint(ks[2], (BATCH, 1), 0, 4096, dtype=jnp.int32)
    return {
        "x": jax.random.normal(ks[0], (BATCH, SEQ, D_MODEL), f32),
        "p": jax.random.normal(ks[1], (DEPTH, BATCH, SEQ, PLE_DIM), f32),
        "positions": offset + jnp.arange(SEQ, dtype=jnp.int32)[None, :],
        "pre_norm_g": gain(ks[3], (DEPTH, D_MODEL)),
        "post_norm_g": gain(ks[4], (DEPTH, D_MODEL)),
        "ret_w_in": w(ks[5], (N_RET_LAYERS, D_MODEL, ret_in_w), D_MODEL),
        "ret_gn_g": gain(ks[6], (N_RET_LAYERS, RET_WIDTH)),
        "ret_w_out": w(ks[7], (N_RET_LAYERS, RET_WIDTH, D_MODEL), RET_WIDTH),
        "mla_w_in": w(ks[8], (N_MLA_LAYERS, D_MODEL, mla_in_w), D_MODEL),
        "mla_q_norm_g": gain(ks[9], (N_MLA_LAYERS, Q_LORA)),
        "mla_kv_norm_g": gain(ks[10], (N_MLA_LAYERS, KV_LORA)),
        "mla_w_uq": w(ks[11], (N_MLA_LAYERS, Q_LORA, MLA_HEADS * (MLA_NOPE + MLA_ROPE)), Q_LORA),
        "mla_w_uk": w(ks[12], (N_MLA_LAYERS, KV_LORA, MLA_HEADS, MLA_NOPE), KV_LORA),
        "mla_w_uv": w(ks[13], (N_MLA_LAYERS, KV_LORA, MLA_HEADS, MLA_V), KV_LORA),
        "mla_w_out": w(ks[14], (N_MLA_LAYERS, MLA_WIDTH, D_MODEL), MLA_WIDTH),
        "ple_w_proj": w(ks[15], (DEPTH, PLE_DIM, D_MODEL), PLE_DIM),
        "ple_w_gate": w(ks[16], (DEPTH, D_MODEL, D_MODEL), D_MODEL),
    }


def reference(x, p, positions, pre_norm_g, post_norm_g, ret_w_in, ret_gn_g, ret_w_out,
              mla_w_in, mla_q_norm_g, mla_kv_norm_g, mla_w_uq, mla_w_uk, mla_w_uv, mla_w_out,
              ple_w_proj, ple_w_gate):
    for i in range(DEPTH):
        h = rms_norm(x, pre_norm_g[i])
        j = i // N_MIXERS
        if i % N_MIXERS == 0:
            y = retention_branch(h, positions, ret_w_in[j], ret_gn_g[j], ret_w_out[j])
        else:
            y = mla_branch(h, positions, mla_w_in[j], mla_q_norm_g[j], mla_kv_norm_g[j],
                           mla_w_uq[j], mla_w_uk[j], mla_w_uv[j], mla_w_out[j])
        x = x + rms_norm(y, post_norm_g[i])
        x = x + (p[i] @ ple_w_proj[i]) * jax.nn.sigmoid(x @ ple_w_gate[i])
    return x
```

```python
import functools
import math

import jax
import jax.numpy as jnp
from jax import lax
from jax.experimental import pallas as pl
from jax.experimental.pallas import tpu as pltpu

D_MODEL = 1024
DEPTH = 4
ROPE_BASE = 10000.0
NORM_EPS = 1e-6

RET_HEADS = 4
RET_QK = 256
RET_V = 512
RET_WIDTH = RET_HEADS * RET_V
RET_CHUNK = 128
RET_LOG_G = tuple(math.log(1.0 - 2.0 ** (-5.0 - h)) for h in range(RET_HEADS))

MLA_HEADS = 16
MLA_NOPE = 128
MLA_ROPE = 64
MLA_V = 128
Q_LORA = 768
KV_LORA = 256
MLA_WIDTH = MLA_HEADS * MLA_V
MLA_QK_PAD = 256
PLE_DIM = 256

LANES = 128
VMEM_LIMIT = 56 * 1024 * 1024

BF16 = jnp.bfloat16
F32 = jnp.float32

_NT = (((1,), (1,)), ((), ()))
_TN = (((0,), (0,)), ((), ()))


def _dot(a, b):
    return jnp.dot(a, b, preferred_element_type=F32)


def _rms(x, g):
    return x * lax.rsqrt(jnp.mean(x * x, axis=-1, keepdims=True) + NORM_EPS) * g


def _silu(x):
    return x * jax.nn.sigmoid(x)


def _params(*sem):
    return pltpu.CompilerParams(dimension_semantics=sem, vmem_limit_bytes=VMEM_LIMIT)


def _row_spec(tm, width):
    return pl.BlockSpec((tm, width), lambda i: (i, 0))


def _full_spec(shape):
    return pl.BlockSpec(shape, lambda *_: (0,) * len(shape))


def _rope_table_kernel(pos_ref, fret_ref, fmla_ref, cos_ref, sin_ref, tab_ref):
    pos = pos_ref[...].astype(F32)
    ang = pos * fret_ref[...]
    cos_ref[...] = jnp.cos(ang)
    sin_ref[...] = jnp.sin(ang)
    ang2 = pos * fmla_ref[...]
    lane = lax.broadcasted_iota(jnp.int32, ang2.shape, 1)
    tab_ref[...] = jnp.where(lane < 2 * (MLA_ROPE // 2), jnp.cos(ang2), jnp.sin(ang2))


def _rope_tables(positions):
    T = positions.size
    ts = min(1024, T)
    half_r = RET_QK // 2
    half_m = MLA_ROPE // 2
    f_ret = (ROPE_BASE ** (-jnp.arange(half_r, dtype=F32) / half_r)).reshape(1, half_r)
    f_mla = ROPE_BASE ** (-jnp.arange(half_m, dtype=F32) / half_m)
    f_mla = jnp.tile(f_mla, LANES // half_m).reshape(1, LANES)
    out = jax.ShapeDtypeStruct((T, LANES), F32)
    return pl.pallas_call(
        _rope_table_kernel,
        grid=(T // ts,),
        in_specs=[_row_spec(ts, 1), _full_spec((1, LANES)), _full_spec((1, LANES))],
        out_specs=[_row_spec(ts, LANES)] * 3,
        out_shape=[out, out, out],
        compiler_params=_params("parallel"),
        name="rope_tables",
    )(positions.reshape(T, 1), f_ret, f_mla)


def _ret_in_kernel(x_ref, g_ref, w_ref, cos_ref, sin_ref, q_ref, k_ref, v_ref, sg_ref):
    h = _rms(x_ref[...], g_ref[...]).astype(BF16)
    cos = cos_ref[...]
    sin = sin_ref[...]
    half = RET_QK // 2
    k_scale = RET_QK ** -0.5
    for j in range(2 * RET_HEADS):
        z = _dot(h, w_ref[:, j * RET_QK:(j + 1) * RET_QK])
        x1 = z[:, :half]
        x2 = z[:, half:]
        r1 = x1 * cos - x2 * sin
        r2 = x2 * cos + x1 * sin
        if j < RET_HEADS:
            q_ref[:, j * RET_QK:j * RET_QK + half] = r1.astype(BF16)
            q_ref[:, j * RET_QK + half:(j + 1) * RET_QK] = r2.astype(BF16)
        else:
            c0 = (j - RET_HEADS) * RET_QK
            k_ref[:, c0:c0 + half] = (r1 * k_scale).astype(BF16)
            k_ref[:, c0 + half:c0 + RET_QK] = (r2 * k_scale).astype(BF16)
    v0 = 2 * RET_HEADS * RET_QK
    g0 = v0 + RET_WIDTH
    for c in range(RET_HEADS):
        cs = slice(c * RET_V, (c + 1) * RET_V)
        v_ref[:, cs] = _dot(h, w_ref[:, v0 + c * RET_V:v0 + (c + 1) * RET_V]).astype(BF16)
        gate = _dot(h, w_ref[:, g0 + c * RET_V:g0 + (c + 1) * RET_V])
        sg_ref[:, cs] = _silu(gate).astype(BF16)


def _ret_in(x, g, w, cos, sin, tm):
    T = x.shape[0]
    qk_w = RET_HEADS * RET_QK
    return pl.pallas_call(
        _ret_in_kernel,
        grid=(T // tm,),
        in_specs=[_row_spec(tm, D_MODEL), _full_spec((1, D_MODEL)), _full_spec(w.shape),
                  _row_spec(tm, LANES), _row_spec(tm, LANES)],
        out_specs=[_row_spec(tm, qk_w), _row_spec(tm, qk_w),
                   _row_spec(tm, RET_WIDTH), _row_spec(tm, RET_WIDTH)],
        out_shape=[jax.ShapeDtypeStruct((T, qk_w), BF16), jax.ShapeDtypeStruct((T, qk_w), BF16),
                   jax.ShapeDtypeStruct((T, RET_WIDTH), BF16),
                   jax.ShapeDtypeStruct((T, RET_WIDTH), BF16)],
        compiler_params=_params("parallel"),
        name="ret_in",
    )(x, g, w, cos, sin)


def _ret_mix_kernel(q_ref, k_ref, v_ref, sg_ref, gn_ref, o_ref,
                    state_ref, dmask_ref, qdec_ref, kdec_ref, *, n_chunks):
    C = RET_CHUNK

    @pl.when(pl.program_id(1) == 0)
    def _init():
        state_ref[...] = jnp.zeros_like(state_ref)
        row = lax.broadcasted_iota(jnp.int32, (C, C), 0)
        col = lax.broadcasted_iota(jnp.int32, (C, C), 1)
        rel = (row - col).astype(F32)
        iq = lax.broadcasted_iota(jnp.int32, (C, RET_V), 0).astype(F32)
        ik = lax.broadcasted_iota(jnp.int32, (C, RET_QK), 0).astype(F32)
        for h in range(RET_HEADS):
            lg = RET_LOG_G[h]
            dmask_ref[h] = jnp.where(rel >= 0, jnp.exp(jnp.maximum(rel, 0.0) * lg), 0.0)
            qdec_ref[h] = jnp.exp((iq + 1.0) * lg)
            kdec_ref[h] = jnp.exp((C - 1.0 - ik) * lg)

    for c in range(n_chunks):
        rows = slice(c * C, (c + 1) * C)
        for h in range(RET_HEADS):
            qs = slice(h * RET_QK, (h + 1) * RET_QK)
            vs = slice(h * RET_V, (h + 1) * RET_V)
            qh = q_ref[0, rows, qs]
            kh = k_ref[0, rows, qs]
            vh = v_ref[0, rows, vs]
            a = lax.dot_general(qh, kh, _NT, preferred_element_type=F32) * dmask_ref[h]
            st = state_ref[h]
            o = _dot(a.astype(BF16), vh) + qdec_ref[h] * _dot(qh, st.astype(BF16))
            kd = (kh.astype(F32) * kdec_ref[h]).astype(BF16)
            state_ref[h] = st * math.exp(C * RET_LOG_G[h]) + lax.dot_general(
                kd, vh, _TN, preferred_element_type=F32)
            o = o * lax.rsqrt(jnp.mean(o * o, axis=-1, keepdims=True) + NORM_EPS)
            o = o * gn_ref[:, vs] * sg_ref[0, rows, vs].astype(F32)
            o_ref[0, rows, vs] = o.astype(BF16)


def _ret_mix(q, k, v, sg, gn, n_chunks):
    B, S, _ = q.shape
    rows = n_chunks * RET_CHUNK
    qk_w = RET_HEADS * RET_QK

    def spec(width):
        return pl.BlockSpec((1, rows, width), lambda b, j: (b, j, 0))

    return pl.pallas_call(
        functools.partial(_ret_mix_kernel, n_chunks=n_chunks),
        grid=(B, S // rows),
        in_specs=[spec(qk_w), spec(qk_w), spec(RET_WIDTH), spec(RET_WIDTH),
                  _full_spec((1, RET_WIDTH))],
        out_specs=spec(RET_WIDTH),
        out_shape=jax.ShapeDtypeStruct((B, S, RET_WIDTH), BF16),
        scratch_shapes=[pltpu.VMEM((RET_HEADS, RET_QK, RET_V), F32),
                        pltpu.VMEM((RET_HEADS, RET_CHUNK, RET_CHUNK), F32),
                        pltpu.VMEM((RET_HEADS, RET_CHUNK, RET_V), F32),
                        pltpu.VMEM((RET_HEADS, RET_CHUNK, RET_QK), F32)],
        compiler_params=_params("arbitrary", "arbitrary"),
        name="ret_mix",
    )(q, k, v, sg, gn)


def _mla_in_kernel(x_ref, g_ref, w_ref, qg_ref, kvg_ref, tab_ref, wuk_ref, wuv_ref,
                   cq_ref, k_ref, v_ref, sg_ref):
    h = _rms(x_ref[...], g_ref[...]).astype(BF16)
    o1 = Q_LORA
    o2 = o1 + KV_LORA
    o3 = o2 + LANES
    q_scale = (MLA_NOPE + MLA_ROPE) ** -0.5
    cq_ref[...] = _rms(_dot(h, w_ref[:, :o1]), qg_ref[...] * q_scale).astype(BF16)
    c_kv = _rms(_dot(h, w_ref[:, o1:o2]), kvg_ref[...]).astype(BF16)
    t = _dot(h, w_ref[:, o2:o3]) * tab_ref[...]
    t = t + pltpu.roll(t, MLA_ROPE, 1)
    lane = lax.broadcasted_iota(jnp.int32, t.shape, 1)
    k_tail = jnp.where(lane < MLA_ROPE, t, 0.0).astype(BF16)
    k_nope = _dot(c_kv, wuk_ref[...]).astype(BF16)
    for hd in range(MLA_HEADS):
        c0 = hd * MLA_QK_PAD
        k_ref[:, c0:c0 + MLA_NOPE] = k_nope[:, hd * MLA_NOPE:(hd + 1) * MLA_NOPE]
        k_ref[:, c0 + MLA_NOPE:c0 + MLA_QK_PAD] = k_tail
    v_ref[...] = _dot(c_kv, wuv_ref[...]).astype(BF16)
    chunk = 512
    for c in range(MLA_WIDTH // chunk):
        gate = _dot(h, w_ref[:, o3 + c * chunk:o3 + (c + 1) * chunk])
        sg_ref[:, c * chunk:(c + 1) * chunk] = _silu(gate).astype(BF16)


def _mla_in(x, g, w, qg, kvg, tab, wuk, wuv, tm):
    T = x.shape[0]
    kw = MLA_HEADS * MLA_QK_PAD
    return pl.pallas_call(
        _mla_in_kernel,
        grid=(T // tm,),
        in_specs=[_row_spec(tm, D_MODEL), _full_spec((1, D_MODEL)), _full_spec(w.shape),
                  _full_spec((1, Q_LORA)), _full_spec((1, KV_LORA)), _row_spec(tm, LANES),
                  _full_spec(wuk.shape), _full_spec(wuv.shape)],
        out_specs=[_row_spec(tm, Q_LORA), _row_spec(tm, kw),
                   _row_spec(tm, MLA_WIDTH), _row_spec(tm, MLA_WIDTH)],
        out_shape=[jax.ShapeDtypeStruct((T, Q_LORA), BF16), jax.ShapeDtypeStruct((T, kw), BF16),
                   jax.ShapeDtypeStruct((T, MLA_WIDTH), BF16),
                   jax.ShapeDtypeStruct((T, MLA_WIDTH), BF16)],
        compiler_params=_params("parallel"),
        name="mla_in",
    )(x, g, w, qg, kvg, tab, wuk, wuv)


def _mla_q_kernel(cq_ref, w_ref, tab_ref, q_ref):
    cq = cq_ref[...]
    tab = tab_ref[...]
    for hd in range(MLA_HEADS):
        c0 = hd * MLA_QK_PAD
        z = _dot(cq, w_ref[:, c0:c0 + MLA_QK_PAD])
        q_ref[:, c0:c0 + MLA_NOPE] = z[:, :MLA_NOPE].astype(BF16)
        t = z[:, MLA_NOPE:] * tab
        q_ref[:, c0 + MLA_NOPE:c0 + MLA_QK_PAD] = (t + pltpu.roll(t, MLA_ROPE, 1)).astype(BF16)


def _mla_q(cq, w, tab, tm):
    T = cq.shape[0]
    qw = MLA_HEADS * MLA_QK_PAD
    return pl.pallas_call(
        _mla_q_kernel,
        grid=(T // tm,),
        in_specs=[_row_spec(tm, Q_LORA), _full_spec(w.shape), _row_spec(tm, LANES)],
        out_specs=_row_spec(tm, qw),
        out_shape=jax.ShapeDtypeStruct((T, qw), BF16),
        compiler_params=_params("parallel"),
        name="mla_q",
    )(cq, w, tab)


def _attn_kernel(q_ref, k_ref, v_ref, sg_ref, o_ref, *, tq, tk):
    i = pl.program_id(2)
    q = q_ref[0]
    per_q = tq // tk

    def step(j, carry, masked):
        m, l, acc = carry
        start = pl.multiple_of(j * tk, tk)
        k = k_ref[0, pl.ds(start, tk), :]
        v = v_ref[0, pl.ds(start, tk), :]
        s = lax.dot_general(q, k, _NT, preferred_element_type=F32)
        if masked:
            row = i * tq + lax.broadcasted_iota(jnp.int32, s.shape, 0)
            col = start + lax.broadcasted_iota(jnp.int32, s.shape, 1)
            s = jnp.where(col <= row, s, -1e30)
        m_new = jnp.maximum(m, jnp.max(s, axis=1, keepdims=True))
        alpha = jnp.exp(m - m_new)
        p = jnp.exp(s - m_new)
        l = alpha * l + jnp.sum(p, axis=1, keepdims=True)
        acc = alpha * acc + _dot(p.astype(BF16), v)
        return m_new, l, acc

    init = (jnp.full((tq, 1), -jnp.inf, F32), jnp.zeros((tq, 1), F32),
            jnp.zeros((tq, MLA_V), F32))
    n_full = i * per_q
    carry = lax.fori_loop(0, n_full, functools.partial(step, masked=False), init)
    carry = lax.fori_loop(n_full, n_full + per_q, functools.partial(step, masked=True), carry)
    _, l, acc = carry
    o_ref[0] = (acc / l * sg_ref[0].astype(F32)).astype(BF16)


def _attn(q, k, v, sg, tq, tk):
    B, S, _ = q.shape
    return pl.pallas_call(
        functools.partial(_attn_kernel, tq=tq, tk=tk),
        grid=(B, MLA_HEADS, S // tq),
        in_specs=[pl.BlockSpec((1, tq, MLA_QK_PAD), lambda b, h, i: (b, i, h)),
                  pl.BlockSpec((1, S, MLA_QK_PAD), lambda b, h, i: (b, 0, h)),
                  pl.BlockSpec((1, S, MLA_V), lambda b, h, i: (b, 0, h)),
                  pl.BlockSpec((1, tq, MLA_V), lambda b, h, i: (b, i, h))],
        out_specs=pl.BlockSpec((1, tq, MLA_V), lambda b, h, i: (b, i, h)),
        out_shape=jax.ShapeDtypeStruct((B, S, MLA_WIDTH), BF16),
        compiler_params=_params("parallel", "parallel", "arbitrary"),
        name="mla_attn",
    )(q, k, v, sg)


def _out_kernel(og_ref, x_ref, p_ref, wo_ref, pg_ref, wp_ref, wg_ref, out_ref):
    y = _dot(og_ref[...], wo_ref[...])
    x1 = x_ref[...] + _rms(y, pg_ref[...])
    pe = _dot(p_ref[...].astype(BF16), wp_ref[...])
    gt = jax.nn.sigmoid(_dot(x1.astype(BF16), wg_ref[...]))
    out_ref[...] = x1 + pe * gt


def _out(og, x, p, layer, wo, pg, wp, wg, tm):
    T = x.shape[0]
    width = og.shape[1]
    return pl.pallas_call(
        _out_kernel,
        grid=(T // tm,),
        in_specs=[_row_spec(tm, width), _row_spec(tm, D_MODEL),
                  pl.BlockSpec((None, tm, PLE_DIM), lambda i: (layer, i, 0)),
                  _full_spec(wo.shape), _full_spec((1, D_MODEL)),
                  _full_spec(wp.shape), _full_spec(wg.shape)],
        out_specs=_row_spec(tm, D_MODEL),
        out_shape=jax.ShapeDtypeStruct((T, D_MODEL), F32),
        compiler_params=_params("parallel"),
        name="branch_out",
    )(og, x, p, wo, pg, wp, wg)


def _rotate_half_cols(w):
    half = w.shape[-1] // 2
    return jnp.concatenate([-w[..., half:], w[..., :half]], axis=-1)


def _prep_mla_w_in(w):
    o2 = Q_LORA + KV_LORA
    o3 = o2 + MLA_ROPE
    kr = w[:, o2:o3]
    return jnp.concatenate([w[:, :o3], _rotate_half_cols(kr), w[:, o3:]], axis=1).astype(BF16)


def _prep_mla_w_uq(w):
    w3 = w.reshape(Q_LORA, MLA_HEADS, MLA_NOPE + MLA_ROPE)
    rp = w3[:, :, MLA_NOPE:]
    w3 = jnp.concatenate([w3, _rotate_half_cols(rp)], axis=-1)
    return w3.reshape(Q_LORA, MLA_HEADS * MLA_QK_PAD).astype(BF16)


def kernel(x, p, positions, pre_norm_g, post_norm_g, ret_w_in, ret_gn_g, ret_w_out, mla_w_in,
           mla_q_norm_g, mla_kv_norm_g, mla_w_uq, mla_w_uk, mla_w_uv, mla_w_out,
           ple_w_proj, ple_w_gate):
    B, S, D = x.shape
    T = B * S
    tm = min(512, T)
    tq = min(512, S)
    n_chunks = min(4, S // RET_CHUNK)

    cos, sin, tab = _rope_tables(positions)
    xf = x.reshape(T, D)
    pf = p.reshape(DEPTH, T, PLE_DIM)

    for i in range(DEPTH):
        j = i // 2
        g_pre = pre_norm_g[i].reshape(1, D)
        if i % 2 == 0:
            q, k, v, sg = _ret_in(xf, g_pre, ret_w_in[j].astype(BF16), cos, sin, tm)
            og = _ret_mix(q.reshape(B, S, -1), k.reshape(B, S, -1), v.reshape(B, S, -1),
                          sg.reshape(B, S, -1), ret_gn_g[j].reshape(1, RET_WIDTH), n_chunks)
            w_out = ret_w_out[j]
        else:
            wuk = mla_w_uk[j].reshape(KV_LORA, MLA_HEADS * MLA_NOPE).astype(BF16)
            wuv = mla_w_uv[j].reshape(KV_LORA, MLA_WIDTH).astype(BF16)
            cq, k, v, sg = _mla_in(xf, g_pre, _prep_mla_w_in(mla_w_in[j]),
                                   mla_q_norm_g[j].reshape(1, Q_LORA),
                                   mla_kv_norm_g[j].reshape(1, KV_LORA), tab, wuk, wuv, tm)
            q = _mla_q(cq, _prep_mla_w_uq(mla_w_uq[j]), tab, tm)
            og = _attn(q.reshape(B, S, -1), k.reshape(B, S, -1), v.reshape(B, S, -1),
                       sg.reshape(B, S, -1), tq, tq)
            w_out = mla_w_out[j]
        xf = _out(og.reshape(T, -1), xf, pf, i, w_out.astype(BF16),
                  post_norm_g[i].reshape(1, D), ple_w_proj[i].astype(BF16),
                  ple_w_gate[i].astype(BF16), tm)
    return xf.reshape(B, S, D)
```

```python
import functools
import math

import jax
import jax.numpy as jnp
from jax import lax
from jax.experimental import pallas as pl
from jax.experimental.pallas import tpu as pltpu

D_MODEL = 1024
DEPTH = 4
ROPE_BASE = 10000.0
NORM_EPS = 1e-6

RET_HEADS = 4
RET_QK = 256
RET_V = 512
RET_WIDTH = RET_HEADS * RET_V
RET_CHUNK = 128
RET_LOG_G = tuple(math.log(1.0 - 2.0 ** (-5.0 - h)) for h in range(RET_HEADS))

MLA_HEADS = 16
MLA_NOPE = 128
MLA_ROPE = 64
MLA_V = 128
Q_LORA = 768
KV_LORA = 256
MLA_WIDTH = MLA_HEADS * MLA_V
MLA_QK_PAD = 256
MLA_V_AUG = MLA_V + 16
ATTN_HEADS_PER_STEP = 2
PLE_DIM = 256

LANES = 128
VMEM_LIMIT = 56 * 1024 * 1024

BF16 = jnp.bfloat16
F32 = jnp.float32

_NT = (((1,), (1,)), ((), ()))
_TN = (((0,), (0,)), ((), ()))


def _dot(a, b):
    return jnp.dot(a, b, preferred_element_type=F32)


def _rms(x, g):
    return x * lax.rsqrt(jnp.mean(x * x, axis=-1, keepdims=True) + NORM_EPS) * g


def _silu(x):
    return x * jax.nn.sigmoid(x)


def _params(*sem):
    return pltpu.CompilerParams(dimension_semantics=sem, vmem_limit_bytes=VMEM_LIMIT)


def _row_spec(tm, width):
    return pl.BlockSpec((tm, width), lambda i: (i, 0))


def _full_spec(shape):
    return pl.BlockSpec(shape, lambda *_: (0,) * len(shape))


def _rope_table_kernel(pos_ref, posr_ref, fret_ref, fmla_ref, fmlac_ref,
                       cos_ref, sin_ref, tab_ref, tabt_ref):
    pos = pos_ref[...].astype(F32)
    ang = pos * fret_ref[...]
    cos_ref[...] = jnp.cos(ang)
    sin_ref[...] = jnp.sin(ang)
    ang2 = pos * fmla_ref[...]
    lane = lax.broadcasted_iota(jnp.int32, ang2.shape, 1)
    tab_ref[...] = jnp.where(lane < MLA_ROPE, jnp.cos(ang2), jnp.sin(ang2))
    ang3 = fmlac_ref[...] * posr_ref[...].astype(F32)
    row = lax.broadcasted_iota(jnp.int32, ang3.shape, 0)
    tabt_ref[...] = jnp.where(row < MLA_ROPE, jnp.cos(ang3), jnp.sin(ang3))


def _rope_tables(positions):
    T = positions.size
    ts = min(1024, T)
    half_r = RET_QK // 2
    half_m = MLA_ROPE // 2
    f_ret = (ROPE_BASE ** (-jnp.arange(half_r, dtype=F32) / half_r)).reshape(1, half_r)
    f_mla = ROPE_BASE ** (-jnp.arange(half_m, dtype=F32) / half_m)
    f_mla = jnp.tile(f_mla, LANES // half_m)
    out = jax.ShapeDtypeStruct((T, LANES), F32)
    return pl.pallas_call(
        _rope_table_kernel,
        grid=(T // ts,),
        in_specs=[_row_spec(ts, 1), pl.BlockSpec((1, ts), lambda i: (0, i)),
                  _full_spec((1, LANES)), _full_spec((1, LANES)), _full_spec((LANES, 1))],
        out_specs=[_row_spec(ts, LANES)] * 3 + [pl.BlockSpec((LANES, ts), lambda i: (0, i))],
        out_shape=[out, out, out, jax.ShapeDtypeStruct((LANES, T), F32)],
        compiler_params=_params("parallel"),
        name="rope_tables",
    )(positions.reshape(T, 1), positions.reshape(1, T), f_ret,
      f_mla.reshape(1, LANES), f_mla.reshape(LANES, 1))


def _ret_in_kernel(x_ref, g_ref, w_ref, cos_ref, sin_ref, q_ref, k_ref, v_ref, sg_ref):
    h = _rms(x_ref[...], g_ref[...]).astype(BF16)
    cos = cos_ref[...]
    sin = sin_ref[...]
    half = RET_QK // 2
    k_scale = RET_QK ** -0.5
    for j in range(2 * RET_HEADS):
        z = _dot(h, w_ref[:, j * RET_QK:(j + 1) * RET_QK])
        x1 = z[:, :half]
        x2 = z[:, half:]
        r1 = x1 * cos - x2 * sin
        r2 = x2 * cos + x1 * sin
        if j < RET_HEADS:
            q_ref[:, j * RET_QK:j * RET_QK + half] = r1.astype(BF16)
            q_ref[:, j * RET_QK + half:(j + 1) * RET_QK] = r2.astype(BF16)
        else:
            c0 = (j - RET_HEADS) * RET_QK
            k_ref[:, c0:c0 + half] = (r1 * k_scale).astype(BF16)
            k_ref[:, c0 + half:c0 + RET_QK] = (r2 * k_scale).astype(BF16)
    v0 = 2 * RET_HEADS * RET_QK
    g0 = v0 + RET_WIDTH
    for c in range(RET_HEADS):
        cs = slice(c * RET_V, (c + 1) * RET_V)
        v_ref[:, cs] = _dot(h, w_ref[:, v0 + c * RET_V:v0 + (c + 1) * RET_V]).astype(BF16)
        gate = _dot(h, w_ref[:, g0 + c * RET_V:g0 + (c + 1) * RET_V])
        sg_ref[:, cs] = _silu(gate).astype(BF16)


def _ret_in(x, g, w, cos, sin, tm):
    T = x.shape[0]
    qk_w = RET_HEADS * RET_QK
    return pl.pallas_call(
        _ret_in_kernel,
        grid=(T // tm,),
        in_specs=[_row_spec(tm, D_MODEL), _full_spec((1, D_MODEL)), _full_spec(w.shape),
                  _row_spec(tm, LANES), _row_spec(tm, LANES)],
        out_specs=[_row_spec(tm, qk_w), _row_spec(tm, qk_w),
                   _row_spec(tm, RET_WIDTH), _row_spec(tm, RET_WIDTH)],
        out_shape=[jax.ShapeDtypeStruct((T, qk_w), BF16), jax.ShapeDtypeStruct((T, qk_w), BF16),
                   jax.ShapeDtypeStruct((T, RET_WIDTH), BF16),
                   jax.ShapeDtypeStruct((T, RET_WIDTH), BF16)],
        compiler_params=_params("parallel"),
        name="ret_in",
    )(x, g, w, cos, sin)


def _ret_mix_kernel(q_ref, k_ref, v_ref, sg_ref, gn_ref, o_ref,
                    state_ref, dmask_ref, qdec_ref, kdec_ref, *, n_chunks):
    C = RET_CHUNK

    @pl.when(pl.program_id(1) == 0)
    def _init():
        state_ref[...] = jnp.zeros_like(state_ref)
        row = lax.broadcasted_iota(jnp.int32, (C, C), 0)
        col = lax.broadcasted_iota(jnp.int32, (C, C), 1)
        rel = (row - col).astype(F32)
        iq = lax.broadcasted_iota(jnp.int32, (C, RET_V), 0).astype(F32)
        ik = lax.broadcasted_iota(jnp.int32, (C, RET_QK), 0).astype(F32)
        for h in range(RET_HEADS):
            lg = RET_LOG_G[h]
            dmask_ref[h] = jnp.where(rel >= 0, jnp.exp(jnp.maximum(rel, 0.0) * lg), 0.0)
            qdec_ref[h] = jnp.exp((iq + 1.0) * lg)
            kdec_ref[h] = jnp.exp((C - 1.0 - ik) * lg)

    for c in range(n_chunks):
        rows = slice(c * C, (c + 1) * C)
        for h in range(RET_HEADS):
            qs = slice(h * RET_QK, (h + 1) * RET_QK)
            vs = slice(h * RET_V, (h + 1) * RET_V)
            qh = q_ref[0, rows, qs]
            kh = k_ref[0, rows, qs]
            vh = v_ref[0, rows, vs]
            a = lax.dot_general(qh, kh, _NT, preferred_element_type=F32) * dmask_ref[h]
            st = state_ref[h]
            o = _dot(a.astype(BF16), vh) + qdec_ref[h] * _dot(qh, st.astype(BF16))
            kd = (kh.astype(F32) * kdec_ref[h]).astype(BF16)
            state_ref[h] = st * math.exp(C * RET_LOG_G[h]) + lax.dot_general(
                kd, vh, _TN, preferred_element_type=F32)
            o = o * lax.rsqrt(jnp.mean(o * o, axis=-1, keepdims=True) + NORM_EPS)
            o = o * gn_ref[:, vs] * sg_ref[0, rows, vs].astype(F32)
            o_ref[0, rows, vs] = o.astype(BF16)


def _ret_mix(q, k, v, sg, gn, n_chunks):
    B, S, _ = q.shape
    rows = n_chunks * RET_CHUNK
    qk_w = RET_HEADS * RET_QK

    def spec(width):
        return pl.BlockSpec((1, rows, width), lambda b, j: (b, j, 0))

    return pl.pallas_call(
        functools.partial(_ret_mix_kernel, n_chunks=n_chunks),
        grid=(B, S // rows),
        in_specs=[spec(qk_w), spec(qk_w), spec(RET_WIDTH), spec(RET_WIDTH),
                  _full_spec((1, RET_WIDTH))],
        out_specs=spec(RET_WIDTH),
        out_shape=jax.ShapeDtypeStruct((B, S, RET_WIDTH), BF16),
        scratch_shapes=[pltpu.VMEM((RET_HEADS, RET_QK, RET_V), F32),
                        pltpu.VMEM((RET_HEADS, RET_CHUNK, RET_CHUNK), F32),
                        pltpu.VMEM((RET_HEADS, RET_CHUNK, RET_V), F32),
                        pltpu.VMEM((RET_HEADS, RET_CHUNK, RET_QK), F32)],
        compiler_params=_params("arbitrary", "arbitrary"),
        name="ret_mix",
    )(q, k, v, sg, gn)


def _mla_in_kernel(x_ref, g_ref, w_ref, qg_ref, kvg_ref, tab_ref, wuk_ref, wuv_ref,
                   cq_ref, k_ref, v_ref, sg_ref):
    h = _rms(x_ref[...], g_ref[...]).astype(BF16)
    o1 = Q_LORA
    o2 = o1 + KV_LORA
    o3 = o2 + LANES
    q_scale = (MLA_NOPE + MLA_ROPE) ** -0.5 * math.log2(math.e)
    cq_ref[...] = _rms(_dot(h, w_ref[:, :o1]), qg_ref[...] * q_scale).astype(BF16)
    c_kv = _rms(_dot(h, w_ref[:, o1:o2]), kvg_ref[...]).astype(BF16)
    t = _dot(h, w_ref[:, o2:o3]) * tab_ref[...]
    t = t + pltpu.roll(t, MLA_ROPE, 1)
    lane = lax.broadcasted_iota(jnp.int32, t.shape, 1)
    k_tail = jnp.where(lane < MLA_ROPE, t, 0.0).astype(BF16)
    k_nope = _dot(c_kv, wuk_ref[...]).astype(BF16)
    for hd in range(MLA_HEADS):
        c0 = hd * MLA_QK_PAD
        k_ref[:, c0:c0 + MLA_NOPE] = k_nope[:, hd * MLA_NOPE:(hd + 1) * MLA_NOPE]
        k_ref[:, c0 + MLA_NOPE:c0 + MLA_QK_PAD] = k_tail
    vt = lax.dot_general(wuv_ref[...], c_kv, _NT, preferred_element_type=F32).astype(BF16)
    ones = jnp.ones((MLA_V_AUG - MLA_V, vt.shape[1]), BF16)
    for hd in range(MLA_HEADS):
        v_ref[0, hd, :MLA_V, :] = vt[hd * MLA_V:(hd + 1) * MLA_V]
        v_ref[0, hd, MLA_V:, :] = ones
    chunk = 512
    for c in range(MLA_WIDTH // chunk):
        gate = _dot(h, w_ref[:, o3 + c * chunk:o3 + (c + 1) * chunk])
        sg_ref[:, c * chunk:(c + 1) * chunk] = _silu(gate).astype(BF16)


def _mla_in(x, g, w, qg, kvg, tab, wuk, wuv, tm):
    T = x.shape[0]
    kw = MLA_HEADS * MLA_QK_PAD
    return pl.pallas_call(
        _mla_in_kernel,
        grid=(T // tm,),
        in_specs=[_row_spec(tm, D_MODEL), _full_spec((1, D_MODEL)), _full_spec(w.shape),
                  _full_spec((1, Q_LORA)), _full_spec((1, KV_LORA)), _row_spec(tm, LANES),
                  _full_spec(wuk.shape), _full_spec(wuv.shape)],
        out_specs=[_row_spec(tm, Q_LORA), _row_spec(tm, kw),
                   pl.BlockSpec((1, MLA_HEADS, MLA_V_AUG, tm), lambda i: (i, 0, 0, 0)),
                   _row_spec(tm, MLA_WIDTH)],
        out_shape=[jax.ShapeDtypeStruct((T, Q_LORA), BF16), jax.ShapeDtypeStruct((T, kw), BF16),
                   jax.ShapeDtypeStruct((T // tm, MLA_HEADS, MLA_V_AUG, tm), BF16),
                   jax.ShapeDtypeStruct((T, MLA_WIDTH), BF16)],
        compiler_params=_params("parallel"),
        name="mla_in",
    )(x, g, w, qg, kvg, tab, wuk, wuv)


def _mla_q_kernel(cq_ref, w_ref, tabt_ref, q_ref):
    cq = cq_ref[...]
    tabt = tabt_ref[...]
    group = 4
    rows = group * MLA_QK_PAD
    for g in range(MLA_HEADS // group):
        z = lax.dot_general(w_ref[g * rows:(g + 1) * rows, :], cq, _NT,
                            preferred_element_type=F32)
        for r in range(group):
            hd = g * group + r
            r0 = r * MLA_QK_PAD
            q_ref[0, hd, :MLA_NOPE, :] = z[r0:r0 + MLA_NOPE].astype(BF16)
            t = z[r0 + MLA_NOPE:r0 + MLA_QK_PAD] * tabt
            q_ref[0, hd, MLA_NOPE:MLA_NOPE + MLA_ROPE, :] = (
                t[:MLA_ROPE] + t[MLA_ROPE:]).astype(BF16)
            q_ref[0, hd, MLA_NOPE + MLA_ROPE:, :] = jnp.zeros(
                (MLA_QK_PAD - MLA_NOPE - MLA_ROPE, t.shape[1]), BF16)


def _mla_q(cq, wt, tabt, tq):
    T = cq.shape[0]
    return pl.pallas_call(
        _mla_q_kernel,
        grid=(T // tq,),
        in_specs=[_row_spec(tq, Q_LORA), _full_spec(wt.shape),
                  pl.BlockSpec((LANES, tq), lambda i: (0, i))],
        out_specs=pl.BlockSpec((1, MLA_HEADS, MLA_QK_PAD, tq), lambda i: (i, 0, 0, 0)),
        out_shape=jax.ShapeDtypeStruct((T // tq, MLA_HEADS, MLA_QK_PAD, tq), BF16),
        compiler_params=_params("parallel"),
        name="mla_q",
    )(cq, wt, tabt)


def _attn_kernel(qt_ref, k_ref, vt_ref, sg_ref, o_ref, acc_ref, s0_ref, s1_ref, *, tq, hp):
    i = pl.program_id(2)
    tk = tq // 2
    acc_ref[...] = jnp.zeros_like(acc_ref)

    def scores(a, j, s_ref):
        start = pl.multiple_of(j * tk, tk)
        k = k_ref[0, pl.ds(start, tk), a * MLA_QK_PAD:(a + 1) * MLA_QK_PAD]
        s_ref[a] = _dot(k, qt_ref[0, a])

    def softmax_pv(a, s_ref, vt, m, mask_row0=None):
        s = s_ref[a]
        if mask_row0 is not None:
            krow = mask_row0 + lax.broadcasted_iota(jnp.int32, s.shape, 0)
            qcol = lax.broadcasted_iota(jnp.int32, s.shape, 1)
            s = jnp.where(krow <= qcol, s, -1e30)
        m_new = jnp.maximum(m, jnp.max(s, axis=0, keepdims=True))
        alpha = jnp.exp2(m - m_new)
        p = jnp.exp2((s - m_new).astype(BF16))
        acc_ref[a] = alpha * acc_ref[a] + _dot(vt, p)
        return m_new

    def pair(jj, ms):
        out = []
        for a in range(hp):
            vt = vt_ref[jj, a]
            scores(a, 2 * jj + 1, s1_ref)
            m = softmax_pv(a, s0_ref, vt[:, :tk], ms[a])
            scores(a, 2 * jj + 2, s0_ref)
            out.append(softmax_pv(a, s1_ref, vt[:, tk:], m))
        return tuple(out)

    for a in range(hp):
        scores(a, 0, s0_ref)
    m0 = jnp.full((1, tq), -jnp.inf, F32)
    ms = lax.fori_loop(0, i, pair, (m0,) * hp)
    for a in range(hp):
        vt = vt_ref[i, a]
        scores(a, 2 * i + 1, s1_ref)
        m = softmax_pv(a, s0_ref, vt[:, :tk], ms[a], mask_row0=0)
        softmax_pv(a, s1_ref, vt[:, tk:], m, mask_row0=tk)
        acc = acc_ref[a]
        o = (acc[:MLA_V] / acc[MLA_V:MLA_V + 1]).T
        cs = slice(a * MLA_V, (a + 1) * MLA_V)
        o_ref[0, :, cs] = (o * sg_ref[0, :, cs].astype(F32)).astype(BF16)


def _attn(qt, k, vt, sg, tq, hp):
    B, S, _ = k.shape
    nq = S // tq
    tk = tq // 2
    return pl.pallas_call(
        functools.partial(_attn_kernel, tq=tq, hp=hp),
        grid=(B, MLA_HEADS // hp, nq),
        in_specs=[pl.BlockSpec((1, hp, MLA_QK_PAD, tq), lambda b, h, i: (b * nq + i, h, 0, 0)),
                  pl.BlockSpec((1, S, hp * MLA_QK_PAD), lambda b, h, i: (b, 0, h)),
                  pl.BlockSpec((nq, hp, MLA_V_AUG, tq), lambda b, h, i: (b, h, 0, 0)),
                  pl.BlockSpec((1, tq, hp * MLA_V), lambda b, h, i: (b, i, h))],
        out_specs=pl.BlockSpec((1, tq, hp * MLA_V), lambda b, h, i: (b, i, h)),
        out_shape=jax.ShapeDtypeStruct((B, S, MLA_WIDTH), BF16),
        scratch_shapes=[pltpu.VMEM((hp, MLA_V_AUG, tq), F32),
                        pltpu.VMEM((hp, tk, tq), F32), pltpu.VMEM((hp, tk, tq), F32)],
        compiler_params=_params("parallel", "parallel", "arbitrary"),
        name="mla_attn",
    )(qt, k, vt, sg)


def _out_kernel(og_ref, x_ref, p_ref, wo_ref, pg_ref, wp_ref, wg_ref, out_ref):
    y = _dot(og_ref[...], wo_ref[...])
    x1 = x_ref[...] + _rms(y, pg_ref[...])
    pe = _dot(p_ref[...].astype(BF16), wp_ref[...])
    gt = jax.nn.sigmoid(_dot(x1.astype(BF16), wg_ref[...]))
    out_ref[...] = x1 + pe * gt


def _out(og, x, p, layer, wo, pg, wp, wg, tm):
    T = x.shape[0]
    width = og.shape[1]
    return pl.pallas_call(
        _out_kernel,
        grid=(T // tm,),
        in_specs=[_row_spec(tm, width), _row_spec(tm, D_MODEL),
                  pl.BlockSpec((None, tm, PLE_DIM), lambda i: (layer, i, 0)),
                  _full_spec(wo.shape), _full_spec((1, D_MODEL)),
                  _full_spec(wp.shape), _full_spec(wg.shape)],
        out_specs=_row_spec(tm, D_MODEL),
        out_shape=jax.ShapeDtypeStruct((T, D_MODEL), F32),
        compiler_params=_params("parallel"),
        name="branch_out",
    )(og, x, p, wo, pg, wp, wg)


def _rotate_half_cols(w):
    half = w.shape[-1] // 2
    return jnp.concatenate([-w[..., half:], w[..., :half]], axis=-1)


def _prep_mla_w_in(w):
    o2 = Q_LORA + KV_LORA
    o3 = o2 + MLA_ROPE
    kr = w[:, o2:o3]
    return jnp.concatenate([w[:, :o3], _rotate_half_cols(kr), w[:, o3:]], axis=1).astype(BF16)


def _prep_mla_w_uq(w):
    w3 = w.reshape(Q_LORA, MLA_HEADS, MLA_NOPE + MLA_ROPE)
    rp = w3[:, :, MLA_NOPE:]
    w3 = jnp.concatenate([w3, _rotate_half_cols(rp)], axis=-1)
    return w3.reshape(Q_LORA, MLA_HEADS * MLA_QK_PAD).T.astype(BF16)


def kernel(x, p, positions, pre_norm_g, post_norm_g, ret_w_in, ret_gn_g, ret_w_out, mla_w_in,
           mla_q_norm_g, mla_kv_norm_g, mla_w_uq, mla_w_uk, mla_w_uv, mla_w_out,
           ple_w_proj, ple_w_gate):
    B, S, D = x.shape
    T = B * S
    tm = min(512, T)
    tq = tm
    assert S % tq == 0
    n_chunks = min(4, S // RET_CHUNK)

    cos, sin, tab, tabt = _rope_tables(positions)
    xf = x.reshape(T, D)
    pf = p.reshape(DEPTH, T, PLE_DIM)

    for i in range(DEPTH):
        j = i // 2
        g_pre = pre_norm_g[i].reshape(1, D)
        if i % 2 == 0:
            q, k, v, sg = _ret_in(xf, g_pre, ret_w_in[j].astype(BF16), cos, sin, tm)
            og = _ret_mix(q.reshape(B, S, -1), k.reshape(B, S, -1), v.reshape(B, S, -1),
                          sg.reshape(B, S, -1), ret_gn_g[j].reshape(1, RET_WIDTH), n_chunks)
            w_out = ret_w_out[j]
        else:
            wuk = mla_w_uk[j].reshape(KV_LORA, MLA_HEADS * MLA_NOPE).astype(BF16)
            wuv_t = mla_w_uv[j].reshape(KV_LORA, MLA_WIDTH).T.astype(BF16)
            cq, k, vt, sg = _mla_in(xf, g_pre, _prep_mla_w_in(mla_w_in[j]),
                                    mla_q_norm_g[j].reshape(1, Q_LORA),
                                    mla_kv_norm_g[j].reshape(1, KV_LORA), tab, wuk, wuv_t, tm)
            qt = _mla_q(cq, _prep_mla_w_uq(mla_w_uq[j]), tabt, tq)
            og = _attn(qt, k.reshape(B, S, -1), vt, sg.reshape(B, S, -1), tq, ATTN_HEADS_PER_STEP)
            w_out = mla_w_out[j]
        xf = _out(og.reshape(T, -1), xf, pf, i, w_out.astype(BF16),
                  post_norm_g[i].reshape(1, D), ple_w_proj[i].astype(BF16),
                  ple_w_gate[i].astype(BF16), tm)
    return xf.reshape(B, S, D)
```

```python
import functools
import math

import jax
import jax.numpy as jnp
from jax import lax
from jax.experimental import pallas as pl
from jax.experimental.pallas import tpu as pltpu

D_MODEL = 1024
DEPTH = 4
ROPE_BASE = 10000.0
NORM_EPS = 1e-6

RET_HEADS = 4
RET_QK = 256
RET_V = 512
RET_WIDTH = RET_HEADS * RET_V
RET_CHUNK = 128
RET_LOG_G = tuple(math.log(1.0 - 2.0 ** (-5.0 - h)) for h in range(RET_HEADS))

MLA_HEADS = 16
MLA_NOPE = 128
MLA_ROPE = 64
MLA_V = 128
Q_LORA = 768
KV_LORA = 256
MLA_WIDTH = MLA_HEADS * MLA_V
MLA_QK_PAD = 256
MLA_V_AUG = MLA_V + 16
ATTN_HEADS_PER_STEP = 2
PLE_DIM = 256

LANES = 128
VMEM_LIMIT = 56 * 1024 * 1024

BF16 = jnp.bfloat16
F32 = jnp.float32

_NT = (((1,), (1,)), ((), ()))
_TN = (((0,), (0,)), ((), ()))


def _dot(a, b):
    return jnp.dot(a, b, preferred_element_type=F32)


def _rms(x, g):
    return x * lax.rsqrt(jnp.mean(x * x, axis=-1, keepdims=True) + NORM_EPS) * g


def _silu(x):
    return x * jax.nn.sigmoid(x)


def _params(*sem):
    return pltpu.CompilerParams(dimension_semantics=sem, vmem_limit_bytes=VMEM_LIMIT)


def _row_spec(tm, width):
    return pl.BlockSpec((tm, width), lambda i: (i, 0))


def _full_spec(shape):
    return pl.BlockSpec(shape, lambda *_: (0,) * len(shape))


def _rope_table_kernel(pos_ref, posr_ref, fret_ref, fmla_ref, fmlac_ref,
                       cos_ref, sin_ref, tab_ref, tabt_ref):
    pos = pos_ref[...].astype(F32)
    ang = pos * fret_ref[...]
    cos_ref[...] = jnp.cos(ang)
    sin_ref[...] = jnp.sin(ang)
    ang2 = pos * fmla_ref[...]
    lane = lax.broadcasted_iota(jnp.int32, ang2.shape, 1)
    tab_ref[...] = jnp.where(lane < MLA_ROPE, jnp.cos(ang2), jnp.sin(ang2))
    ang3 = fmlac_ref[...] * posr_ref[...].astype(F32)
    row = lax.broadcasted_iota(jnp.int32, ang3.shape, 0)
    tabt_ref[...] = jnp.where(row < MLA_ROPE, jnp.cos(ang3), jnp.sin(ang3))


def _rope_tables(positions):
    T = positions.size
    ts = min(1024, T)
    half_r = RET_QK // 2
    half_m = MLA_ROPE // 2
    f_ret = (ROPE_BASE ** (-jnp.arange(half_r, dtype=F32) / half_r)).reshape(1, half_r)
    f_mla = ROPE_BASE ** (-jnp.arange(half_m, dtype=F32) / half_m)
    f_mla = jnp.tile(f_mla, LANES // half_m)
    out = jax.ShapeDtypeStruct((T, LANES), F32)
    return pl.pallas_call(
        _rope_table_kernel,
        grid=(T // ts,),
        in_specs=[_row_spec(ts, 1), pl.BlockSpec((1, ts), lambda i: (0, i)),
                  _full_spec((1, LANES)), _full_spec((1, LANES)), _full_spec((LANES, 1))],
        out_specs=[_row_spec(ts, LANES)] * 3 + [pl.BlockSpec((LANES, ts), lambda i: (0, i))],
        out_shape=[out, out, out, jax.ShapeDtypeStruct((LANES, T), F32)],
        compiler_params=_params("parallel"),
        name="rope_tables",
    )(positions.reshape(T, 1), positions.reshape(1, T), f_ret,
      f_mla.reshape(1, LANES), f_mla.reshape(LANES, 1))


def _ret_in_kernel(x_ref, g_ref, w_ref, cos_ref, sin_ref, q_ref, k_ref, v_ref, sg_ref):
    h = _rms(x_ref[...], g_ref[...]).astype(BF16)
    cos = cos_ref[...]
    sin = sin_ref[...]
    half = RET_QK // 2
    k_scale = RET_QK ** -0.5
    for j in range(2 * RET_HEADS):
        z = _dot(h, w_ref[:, j * RET_QK:(j + 1) * RET_QK])
        x1 = z[:, :half]
        x2 = z[:, half:]
        r1 = x1 * cos - x2 * sin
        r2 = x2 * cos + x1 * sin
        if j < RET_HEADS:
            q_ref[:, j * RET_QK:j * RET_QK + half] = r1.astype(BF16)
            q_ref[:, j * RET_QK + half:(j + 1) * RET_QK] = r2.astype(BF16)
        else:
            c0 = (j - RET_HEADS) * RET_QK
            k_ref[:, c0:c0 + half] = (r1 * k_scale).astype(BF16)
            k_ref[:, c0 + half:c0 + RET_QK] = (r2 * k_scale).astype(BF16)
    v0 = 2 * RET_HEADS * RET_QK
    g0 = v0 + RET_WIDTH
    for c in range(RET_HEADS):
        cs = slice(c * RET_V, (c + 1) * RET_V)
        v_ref[:, cs] = _dot(h, w_ref[:, v0 + c * RET_V:v0 + (c + 1) * RET_V]).astype(BF16)
        gate = _dot(h, w_ref[:, g0 + c * RET_V:g0 + (c + 1) * RET_V])
        sg_ref[:, cs] = _silu(gate).astype(BF16)


def _ret_in(x, g, w, cos, sin, tm):
    T = x.shape[0]
    qk_w = RET_HEADS * RET_QK
    return pl.pallas_call(
        _ret_in_kernel,
        grid=(T // tm,),
        in_specs=[_row_spec(tm, D_MODEL), _full_spec((1, D_MODEL)), _full_spec(w.shape),
                  _row_spec(tm, LANES), _row_spec(tm, LANES)],
        out_specs=[_row_spec(tm, qk_w), _row_spec(tm, qk_w),
                   _row_spec(tm, RET_WIDTH), _row_spec(tm, RET_WIDTH)],
        out_shape=[jax.ShapeDtypeStruct((T, qk_w), BF16), jax.ShapeDtypeStruct((T, qk_w), BF16),
                   jax.ShapeDtypeStruct((T, RET_WIDTH), BF16),
                   jax.ShapeDtypeStruct((T, RET_WIDTH), BF16)],
        compiler_params=_params("parallel"),
        name="ret_in",
    )(x, g, w, cos, sin)


def _ret_mix_kernel(q_ref, k_ref, v_ref, sg_ref, gn_ref, o_ref,
                    state_ref, dmask_ref, qdec_ref, kdec_ref, *, n_chunks):
    C = RET_CHUNK

    @pl.when(pl.program_id(1) == 0)
    def _init():
        state_ref[...] = jnp.zeros_like(state_ref)
        row = lax.broadcasted_iota(jnp.int32, (C, C), 0)
        col = lax.broadcasted_iota(jnp.int32, (C, C), 1)
        rel = (row - col).astype(F32)
        iq = lax.broadcasted_iota(jnp.int32, (C, RET_V), 0).astype(F32)
        ik = lax.broadcasted_iota(jnp.int32, (C, RET_QK), 0).astype(F32)
        for h in range(RET_HEADS):
            lg = RET_LOG_G[h]
            dmask_ref[h] = jnp.where(rel >= 0, jnp.exp(jnp.maximum(rel, 0.0) * lg), 0.0)
            qdec_ref[h] = jnp.exp((iq + 1.0) * lg)
            kdec_ref[h] = jnp.exp((C - 1.0 - ik) * lg)

    for c in range(n_chunks):
        rows = slice(c * C, (c + 1) * C)
        for h in range(RET_HEADS):
            qs = slice(h * RET_QK, (h + 1) * RET_QK)
            vs = slice(h * RET_V, (h + 1) * RET_V)
            qh = q_ref[0, rows, qs]
            kh = k_ref[0, rows, qs]
            vh = v_ref[0, rows, vs]
            a = lax.dot_general(qh, kh, _NT, preferred_element_type=F32) * dmask_ref[h]
            st = state_ref[h]
            o = _dot(a.astype(BF16), vh) + qdec_ref[h] * _dot(qh, st.astype(BF16))
            kd = (kh.astype(F32) * kdec_ref[h]).astype(BF16)
            state_ref[h] = st * math.exp(C * RET_LOG_G[h]) + lax.dot_general(
                kd, vh, _TN, preferred_element_type=F32)
            o = o * lax.rsqrt(jnp.mean(o * o, axis=-1, keepdims=True) + NORM_EPS)
            o = o * gn_ref[:, vs] * sg_ref[0, rows, vs].astype(F32)
            o_ref[0, rows, vs] = o.astype(BF16)


def _ret_mix(q, k, v, sg, gn, n_chunks):
    B, S, _ = q.shape
    rows = n_chunks * RET_CHUNK
    qk_w = RET_HEADS * RET_QK

    def spec(width):
        return pl.BlockSpec((1, rows, width), lambda b, j: (b, j, 0))

    return pl.pallas_call(
        functools.partial(_ret_mix_kernel, n_chunks=n_chunks),
        grid=(B, S // rows),
        in_specs=[spec(qk_w), spec(qk_w), spec(RET_WIDTH), spec(RET_WIDTH),
                  _full_spec((1, RET_WIDTH))],
        out_specs=spec(RET_WIDTH),
        out_shape=jax.ShapeDtypeStruct((B, S, RET_WIDTH), BF16),
        scratch_shapes=[pltpu.VMEM((RET_HEADS, RET_QK, RET_V), F32),
                        pltpu.VMEM((RET_HEADS, RET_CHUNK, RET_CHUNK), F32),
                        pltpu.VMEM((RET_HEADS, RET_CHUNK, RET_V), F32),
                        pltpu.VMEM((RET_HEADS, RET_CHUNK, RET_QK), F32)],
        compiler_params=_params("arbitrary", "arbitrary"),
        name="ret_mix",
    )(q, k, v, sg, gn)


def _mla_in_kernel(x_ref, g_ref, w_ref, qg_ref, kvg_ref, tab_ref, wuk_ref, wuv_ref,
                   cq_ref, k_ref, v_ref, sg_ref):
    h = _rms(x_ref[...], g_ref[...]).astype(BF16)
    o1 = Q_LORA
    o2 = o1 + KV_LORA
    o3 = o2 + LANES
    q_scale = (MLA_NOPE + MLA_ROPE) ** -0.5 * math.log2(math.e)
    cq_ref[...] = _rms(_dot(h, w_ref[:, :o1]), qg_ref[...] * q_scale).astype(BF16)
    c_kv = _rms(_dot(h, w_ref[:, o1:o2]), kvg_ref[...]).astype(BF16)
    t = _dot(h, w_ref[:, o2:o3]) * tab_ref[...]
    t = t + pltpu.roll(t, MLA_ROPE, 1)
    lane = lax.broadcasted_iota(jnp.int32, t.shape, 1)
    k_tail = jnp.where(lane < MLA_ROPE, t, 0.0).astype(BF16)
    k_nope = _dot(c_kv, wuk_ref[...]).astype(BF16)
    for hd in range(MLA_HEADS):
        c0 = hd * MLA_QK_PAD
        k_ref[:, c0:c0 + MLA_NOPE] = k_nope[:, hd * MLA_NOPE:(hd + 1) * MLA_NOPE]
        k_ref[:, c0 + MLA_NOPE:c0 + MLA_QK_PAD] = k_tail
    vt = lax.dot_general(wuv_ref[...], c_kv, _NT, preferred_element_type=F32).astype(BF16)
    ones = jnp.ones((MLA_V_AUG - MLA_V, vt.shape[1]), BF16)
    for hd in range(MLA_HEADS):
        v_ref[0, hd, :MLA_V, :] = vt[hd * MLA_V:(hd + 1) * MLA_V]
        v_ref[0, hd, MLA_V:, :] = ones
    chunk = 512
    for c in range(MLA_WIDTH // chunk):
        gate = _dot(h, w_ref[:, o3 + c * chunk:o3 + (c + 1) * chunk])
        sg_ref[:, c * chunk:(c + 1) * chunk] = _silu(gate).astype(BF16)


def _mla_in(x, g, w, qg, kvg, tab, wuk, wuv, tm):
    T = x.shape[0]
    kw = MLA_HEADS * MLA_QK_PAD
    return pl.pallas_call(
        _mla_in_kernel,
        grid=(T // tm,),
        in_specs=[_row_spec(tm, D_MODEL), _full_spec((1, D_MODEL)), _full_spec(w.shape),
                  _full_spec((1, Q_LORA)), _full_spec((1, KV_LORA)), _row_spec(tm, LANES),
                  _full_spec(wuk.shape), _full_spec(wuv.shape)],
        out_specs=[_row_spec(tm, Q_LORA), _row_spec(tm, kw),
                   pl.BlockSpec((1, MLA_HEADS, MLA_V_AUG, tm), lambda i: (i, 0, 0, 0)),
                   _row_spec(tm, MLA_WIDTH)],
        out_shape=[jax.ShapeDtypeStruct((T, Q_LORA), BF16), jax.ShapeDtypeStruct((T, kw), BF16),
                   jax.ShapeDtypeStruct((T // tm, MLA_HEADS, MLA_V_AUG, tm), BF16),
                   jax.ShapeDtypeStruct((T, MLA_WIDTH), BF16)],
        compiler_params=_params("parallel"),
        name="mla_in",
    )(x, g, w, qg, kvg, tab, wuk, wuv)


def _mla_q_kernel(cq_ref, w_ref, tabt_ref, q_ref):
    cq = cq_ref[...]
    tabt = tabt_ref[...]
    group = 4
    rows = group * MLA_QK_PAD
    for g in range(MLA_HEADS // group):
        z = lax.dot_general(w_ref[g * rows:(g + 1) * rows, :], cq, _NT,
                            preferred_element_type=F32)
        for r in range(group):
            hd = g * group + r
            r0 = r * MLA_QK_PAD
            q_ref[0, hd, :MLA_NOPE, :] = z[r0:r0 + MLA_NOPE].astype(BF16)
            t = z[r0 + MLA_NOPE:r0 + MLA_QK_PAD] * tabt
            q_ref[0, hd, MLA_NOPE:MLA_NOPE + MLA_ROPE, :] = (
                t[:MLA_ROPE] + t[MLA_ROPE:]).astype(BF16)
            q_ref[0, hd, MLA_NOPE + MLA_ROPE:, :] = jnp.zeros(
                (MLA_QK_PAD - MLA_NOPE - MLA_ROPE, t.shape[1]), BF16)


def _mla_q(cq, wt, tabt, tq):
    T = cq.shape[0]
    return pl.pallas_call(
        _mla_q_kernel,
        grid=(T // tq,),
        in_specs=[_row_spec(tq, Q_LORA), _full_spec(wt.shape),
                  pl.BlockSpec((LANES, tq), lambda i: (0, i))],
        out_specs=pl.BlockSpec((1, MLA_HEADS, MLA_QK_PAD, tq), lambda i: (i, 0, 0, 0)),
        out_shape=jax.ShapeDtypeStruct((T // tq, MLA_HEADS, MLA_QK_PAD, tq), BF16),
        compiler_params=_params("parallel"),
        name="mla_q",
    )(cq, wt, tabt)


def _attn_kernel(qt_ref, k_ref, vt_ref, sg_ref, o_ref, acc_ref, s0_ref, s1_ref,
                 p0_ref, p1_ref, *, tq, hp):
    i = pl.program_id(2)
    tk = tq // 2
    acc_ref[...] = jnp.zeros_like(acc_ref)

    def stage_a(a, key0, s_ref, mask_row0=None):
        k = k_ref[0, pl.ds(pl.multiple_of(key0, tk), tk), a * MLA_QK_PAD:(a + 1) * MLA_QK_PAD]
        s = _dot(k, qt_ref[0, a])
        if mask_row0 is not None:
            krow = mask_row0 + lax.broadcasted_iota(jnp.int32, s.shape, 0)
            qcol = lax.broadcasted_iota(jnp.int32, s.shape, 1)
            s = jnp.where(krow <= qcol, s, -1e30)
        s_ref[a] = s
        return jnp.max(s, axis=0, keepdims=True)

    def stage_b(a, s_ref, p_ref, m, cm):
        m_new = jnp.maximum(m, cm)
        p_ref[a] = jnp.exp2((s_ref[a] - m_new).astype(BF16))
        return m_new, jnp.exp2(m - m_new)

    def stage_c(a, p_ref, vt, alpha):
        acc_ref[a] = alpha * acc_ref[a] + _dot(vt, p_ref[a])

    state = []
    for a in range(hp):
        cm0 = stage_a(a, i * tq, s0_ref, mask_row0=0)
        cm1 = stage_a(a, i * tq + tk, s1_ref, mask_row0=tk)
        m, alpha0 = stage_b(a, s0_ref, p0_ref, jnp.full((1, tq), -jnp.inf, F32), cm0)
        state.append((m, cm1, alpha0))

    def pair(jj, state):
        blk = 2 * jnp.where(jj == 0, i, jj - 1)
        heads = range(hp)
        ms = [st[0] for st in state]
        cm1s = [st[1] for st in state]
        alpha0s = [st[2] for st in state]
        alpha1s = [None] * hp
        cm0s = [stage_a(a, jj * tq, s0_ref) for a in heads]
        for a in heads:
            stage_c(a, p0_ref, vt_ref[blk, a], alpha0s[a])
        for a in heads:
            ms[a], alpha1s[a] = stage_b(a, s1_ref, p1_ref, ms[a], cm1s[a])
        cm1s = [stage_a(a, jj * tq + tk, s1_ref) for a in heads]
        for a in heads:
            stage_c(a, p1_ref, vt_ref[blk + 1, a], alpha1s[a])
        for a in heads:
            ms[a], alpha0s[a] = stage_b(a, s0_ref, p0_ref, ms[a], cm0s[a])
        return tuple(zip(ms, cm1s, alpha0s))

    state = lax.fori_loop(0, i, pair, tuple(state))
    blk = 2 * jnp.maximum(i - 1, 0)
    for a in range(hp):
        m, cm1, alpha0 = state[a]
        _, alpha1 = stage_b(a, s1_ref, p1_ref, m, cm1)
        stage_c(a, p0_ref, vt_ref[blk, a], alpha0)
        stage_c(a, p1_ref, vt_ref[blk + 1, a], alpha1)
        acc = acc_ref[a]
        o = (acc[:MLA_V] / acc[MLA_V:MLA_V + 1]).T
        cs = slice(a * MLA_V, (a + 1) * MLA_V)
        o_ref[0, :, cs] = (o * sg_ref[0, :, cs].astype(F32)).astype(BF16)


def _attn(qt, k, vt, sg, tq, hp):
    B, S, _ = k.shape
    nq = S // tq
    tk = tq // 2
    return pl.pallas_call(
        functools.partial(_attn_kernel, tq=tq, hp=hp),
        grid=(B, MLA_HEADS // hp, nq),
        in_specs=[pl.BlockSpec((1, hp, MLA_QK_PAD, tq), lambda b, h, i: (b * nq + i, h, 0, 0)),
                  pl.BlockSpec((1, S, hp * MLA_QK_PAD), lambda b, h, i: (b, 0, h)),
                  pl.BlockSpec((S // tk, hp, MLA_V_AUG, tk), lambda b, h, i: (b, h, 0, 0)),
                  pl.BlockSpec((1, tq, hp * MLA_V), lambda b, h, i: (b, i, h))],
        out_specs=pl.BlockSpec((1, tq, hp * MLA_V), lambda b, h, i: (b, i, h)),
        out_shape=jax.ShapeDtypeStruct((B, S, MLA_WIDTH), BF16),
        scratch_shapes=[pltpu.VMEM((hp, MLA_V_AUG, tq), F32),
                        pltpu.VMEM((hp, tk, tq), F32), pltpu.VMEM((hp, tk, tq), F32),
                        pltpu.VMEM((hp, tk, tq), BF16), pltpu.VMEM((hp, tk, tq), BF16)],
        compiler_params=_params("parallel", "parallel", "arbitrary"),
        name="mla_attn",
    )(qt, k, vt, sg)


def _out_kernel(og_ref, x_ref, p_ref, wo_ref, pg_ref, wp_ref, wg_ref, out_ref):
    y = _dot(og_ref[...], wo_ref[...])
    x1 = x_ref[...] + _rms(y, pg_ref[...])
    pe = _dot(p_ref[...].astype(BF16), wp_ref[...])
    gt = jax.nn.sigmoid(_dot(x1.astype(BF16), wg_ref[...]))
    out_ref[...] = x1 + pe * gt


def _out(og, x, p, layer, wo, pg, wp, wg, tm):
    T = x.shape[0]
    width = og.shape[1]
    return pl.pallas_call(
        _out_kernel,
        grid=(T // tm,),
        in_specs=[_row_spec(tm, width), _row_spec(tm, D_MODEL),
                  pl.BlockSpec((None, tm, PLE_DIM), lambda i: (layer, i, 0)),
                  _full_spec(wo.shape), _full_spec((1, D_MODEL)),
                  _full_spec(wp.shape), _full_spec(wg.shape)],
        out_specs=_row_spec(tm, D_MODEL),
        out_shape=jax.ShapeDtypeStruct((T, D_MODEL), F32),
        compiler_params=_params("parallel"),
        name="branch_out",
    )(og, x, p, wo, pg, wp, wg)


def _rotate_half_cols(w):
    half = w.shape[-1] // 2
    return jnp.concatenate([-w[..., half:], w[..., :half]], axis=-1)


def _prep_mla_w_in(w):
    o2 = Q_LORA + KV_LORA
    o3 = o2 + MLA_ROPE
    kr = w[:, o2:o3]
    return jnp.concatenate([w[:, :o3], _rotate_half_cols(kr), w[:, o3:]], axis=1).astype(BF16)


def _prep_mla_w_uq(w):
    w3 = w.reshape(Q_LORA, MLA_HEADS, MLA_NOPE + MLA_ROPE)
    rp = w3[:, :, MLA_NOPE:]
    w3 = jnp.concatenate([w3, _rotate_half_cols(rp)], axis=-1)
    return w3.reshape(Q_LORA, MLA_HEADS * MLA_QK_PAD).T.astype(BF16)


def kernel(x, p, positions, pre_norm_g, post_norm_g, ret_w_in, ret_gn_g, ret_w_out, mla_w_in,
           mla_q_norm_g, mla_kv_norm_g, mla_w_uq, mla_w_uk, mla_w_uv, mla_w_out,
           ple_w_proj, ple_w_gate):
    B, S, D = x.shape
    T = B * S
    tm = min(512, T)
    tq = 2 * tm
    assert S % tq == 0
    n_chunks = min(4, S // RET_CHUNK)

    cos, sin, tab, tabt = _rope_tables(positions)
    xf = x.reshape(T, D)
    pf = p.reshape(DEPTH, T, PLE_DIM)

    for i in range(DEPTH):
        j = i // 2
        g_pre = pre_norm_g[i].reshape(1, D)
        if i % 2 == 0:
            q, k, v, sg = _ret_in(xf, g_pre, ret_w_in[j].astype(BF16), cos, sin, tm)
            og = _ret_mix(q.reshape(B, S, -1), k.reshape(B, S, -1), v.reshape(B, S, -1),
                          sg.reshape(B, S, -1), ret_gn_g[j].reshape(1, RET_WIDTH), n_chunks)
            w_out = ret_w_out[j]
        else:
            wuk = mla_w_uk[j].reshape(KV_LORA, MLA_HEADS * MLA_NOPE).astype(BF16)
            wuv_t = mla_w_uv[j].reshape(KV_LORA, MLA_WIDTH).T.astype(BF16)
            cq, k, vt, sg = _mla_in(xf, g_pre, _prep_mla_w_in(mla_w_in[j]),
                                    mla_q_norm_g[j].reshape(1, Q_LORA),
                                    mla_kv_norm_g[j].reshape(1, KV_LORA), tab, wuk, wuv_t, tm)
            qt = _mla_q(cq, _prep_mla_w_uq(mla_w_uq[j]), tabt, tq)
            og = _attn(qt, k.reshape(B, S, -1), vt, sg.reshape(B, S, -1), tq, ATTN_HEADS_PER_STEP)
            w_out = mla_w_out[j]
        xf = _out(og.reshape(T, -1), xf, pf, i, w_out.astype(BF16),
                  post_norm_g[i].reshape(1, D), ple_w_proj[i].astype(BF16),
                  ple_w_gate[i].astype(BF16), tm)
    return xf.reshape(B, S, D)
```

```python
import functools
import math

import jax
import jax.numpy as jnp
from jax import lax
from jax.experimental import pallas as pl
from jax.experimental.pallas import tpu as pltpu

D_MODEL = 1024
DEPTH = 4
ROPE_BASE = 10000.0
NORM_EPS = 1e-6

RET_HEADS = 4
RET_QK = 256
RET_V = 512
RET_WIDTH = RET_HEADS * RET_V
RET_CHUNK = 256
RET_LOG_G = tuple(math.log(1.0 - 2.0 ** (-5.0 - h)) for h in range(RET_HEADS))

MLA_HEADS = 16
MLA_NOPE = 128
MLA_ROPE = 64
MLA_V = 128
Q_LORA = 768
KV_LORA = 256
MLA_WIDTH = MLA_HEADS * MLA_V
MLA_QK_PAD = 256
MLA_V_AUG = MLA_V + 16
ATTN_HEADS_PER_STEP = 2
PLE_DIM = 256

LANES = 128
VMEM_LIMIT = 56 * 1024 * 1024

BF16 = jnp.bfloat16
F32 = jnp.float32

_NT = (((1,), (1,)), ((), ()))
_TN = (((0,), (0,)), ((), ()))


def _dot(a, b):
    return jnp.dot(a, b, preferred_element_type=F32)


def _rms(x, g):
    return x * lax.rsqrt(jnp.mean(x * x, axis=-1, keepdims=True) + NORM_EPS) * g


def _silu(x):
    return x * jax.nn.sigmoid(x)


def _params(*sem):
    return pltpu.CompilerParams(dimension_semantics=sem, vmem_limit_bytes=VMEM_LIMIT)


def _row_spec(tm, width):
    return pl.BlockSpec((tm, width), lambda i: (i, 0))


def _full_spec(shape):
    return pl.BlockSpec(shape, lambda *_: (0,) * len(shape))


def _rope_table_kernel(pos_ref, posr_ref, fret_ref, fmla_ref, fmlac_ref,
                       cos_ref, sin_ref, tab_ref, tabt_ref):
    pos = pos_ref[...].astype(F32)
    ang = pos * fret_ref[...]
    cos_ref[...] = jnp.cos(ang)
    sin_ref[...] = jnp.sin(ang)
    ang2 = pos * fmla_ref[...]
    lane = lax.broadcasted_iota(jnp.int32, ang2.shape, 1)
    tab_ref[...] = jnp.where(lane < MLA_ROPE, jnp.cos(ang2), jnp.sin(ang2))
    ang3 = fmlac_ref[...] * posr_ref[...].astype(F32)
    row = lax.broadcasted_iota(jnp.int32, ang3.shape, 0)
    tabt_ref[...] = jnp.where(row < MLA_ROPE, jnp.cos(ang3), jnp.sin(ang3))


def _rope_tables(positions):
    T = positions.size
    ts = min(1024, T)
    half_r = RET_QK // 2
    half_m = MLA_ROPE // 2
    f_ret = (ROPE_BASE ** (-jnp.arange(half_r, dtype=F32) / half_r)).reshape(1, half_r)
    f_mla = ROPE_BASE ** (-jnp.arange(half_m, dtype=F32) / half_m)
    f_mla = jnp.tile(f_mla, LANES // half_m)
    out = jax.ShapeDtypeStruct((T, LANES), F32)
    return pl.pallas_call(
        _rope_table_kernel,
        grid=(T // ts,),
        in_specs=[_row_spec(ts, 1), pl.BlockSpec((1, ts), lambda i: (0, i)),
                  _full_spec((1, LANES)), _full_spec((1, LANES)), _full_spec((LANES, 1))],
        out_specs=[_row_spec(ts, LANES)] * 3 + [pl.BlockSpec((LANES, ts), lambda i: (0, i))],
        out_shape=[out, out, out, jax.ShapeDtypeStruct((LANES, T), F32)],
        compiler_params=_params("parallel"),
        name="rope_tables",
    )(positions.reshape(T, 1), positions.reshape(1, T), f_ret,
      f_mla.reshape(1, LANES), f_mla.reshape(LANES, 1))


def _ret_in_kernel(x_ref, g_ref, w_ref, cos_ref, sin_ref, q_ref, k_ref, v_ref, sg_ref):
    h = _rms(x_ref[...], g_ref[...]).astype(BF16)
    cos = cos_ref[...]
    sin = sin_ref[...]
    half = RET_QK // 2
    k_scale = RET_QK ** -0.5
    for j in range(2 * RET_HEADS):
        z = _dot(h, w_ref[:, j * RET_QK:(j + 1) * RET_QK])
        x1 = z[:, :half]
        x2 = z[:, half:]
        r1 = x1 * cos - x2 * sin
        r2 = x2 * cos + x1 * sin
        if j < RET_HEADS:
            q_ref[:, j * RET_QK:j * RET_QK + half] = r1.astype(BF16)
            q_ref[:, j * RET_QK + half:(j + 1) * RET_QK] = r2.astype(BF16)
        else:
            c0 = (j - RET_HEADS) * RET_QK
            k_ref[:, c0:c0 + half] = (r1 * k_scale).astype(BF16)
            k_ref[:, c0 + half:c0 + RET_QK] = (r2 * k_scale).astype(BF16)
    v0 = 2 * RET_HEADS * RET_QK
    g0 = v0 + RET_WIDTH
    for c in range(RET_HEADS):
        cs = slice(c * RET_V, (c + 1) * RET_V)
        v_ref[:, cs] = _dot(h, w_ref[:, v0 + c * RET_V:v0 + (c + 1) * RET_V]).astype(BF16)
        gate = _dot(h, w_ref[:, g0 + c * RET_V:g0 + (c + 1) * RET_V])
        sg_ref[:, cs] = _silu(gate).astype(BF16)


def _ret_in(x, g, w, cos, sin, tm):
    T = x.shape[0]
    qk_w = RET_HEADS * RET_QK
    return pl.pallas_call(
        _ret_in_kernel,
        grid=(T // tm,),
        in_specs=[_row_spec(tm, D_MODEL), _full_spec((1, D_MODEL)), _full_spec(w.shape),
                  _row_spec(tm, LANES), _row_spec(tm, LANES)],
        out_specs=[_row_spec(tm, qk_w), _row_spec(tm, qk_w),
                   _row_spec(tm, RET_WIDTH), _row_spec(tm, RET_WIDTH)],
        out_shape=[jax.ShapeDtypeStruct((T, qk_w), BF16), jax.ShapeDtypeStruct((T, qk_w), BF16),
                   jax.ShapeDtypeStruct((T, RET_WIDTH), BF16),
                   jax.ShapeDtypeStruct((T, RET_WIDTH), BF16)],
        compiler_params=_params("parallel"),
        name="ret_in",
    )(x, g, w, cos, sin)


def _ret_mix_kernel(q_ref, k_ref, v_ref, sg_ref, gn_ref, o_ref,
                    state_ref, dmask_ref, qdec_ref, kdec_ref, *, n_chunks):
    C = RET_CHUNK

    @pl.when(pl.program_id(1) == 0)
    def _init():
        state_ref[...] = jnp.zeros_like(state_ref)
        row = lax.broadcasted_iota(jnp.int32, (C, C), 0)
        col = lax.broadcasted_iota(jnp.int32, (C, C), 1)
        rel = (row - col).astype(F32)
        iq = lax.broadcasted_iota(jnp.int32, (C, RET_V), 0).astype(F32)
        ik = lax.broadcasted_iota(jnp.int32, (C, RET_QK), 0).astype(F32)
        for h in range(RET_HEADS):
            lg = RET_LOG_G[h]
            dmask_ref[h] = jnp.where(rel >= 0, jnp.exp(jnp.maximum(rel, 0.0) * lg), 0.0)
            qdec_ref[h] = jnp.exp((iq + 1.0) * lg)
            kdec_ref[h] = jnp.exp((C - 1.0 - ik) * lg)

    for c in range(n_chunks):
        rows = slice(c * C, (c + 1) * C)
        for h in range(RET_HEADS):
            qs = slice(h * RET_QK, (h + 1) * RET_QK)
            vs = slice(h * RET_V, (h + 1) * RET_V)
            qh = q_ref[0, rows, qs]
            kh = k_ref[0, rows, qs]
            vh = v_ref[0, rows, vs]
            a = lax.dot_general(qh, kh, _NT, preferred_element_type=F32) * dmask_ref[h]
            st = state_ref[h]
            o = _dot(a.astype(BF16), vh) + qdec_ref[h] * _dot(qh, st.astype(BF16))
            kd = (kh.astype(F32) * kdec_ref[h]).astype(BF16)
            state_ref[h] = st * math.exp(C * RET_LOG_G[h]) + lax.dot_general(
                kd, vh, _TN, preferred_element_type=F32)
            o = o * lax.rsqrt(jnp.mean(o * o, axis=-1, keepdims=True) + NORM_EPS)
            o = o * gn_ref[:, vs] * sg_ref[0, rows, vs].astype(F32)
            o_ref[0, rows, vs] = o.astype(BF16)


def _ret_mix(q, k, v, sg, gn, n_chunks):
    B, S, _ = q.shape
    rows = n_chunks * RET_CHUNK
    qk_w = RET_HEADS * RET_QK

    def spec(width):
        return pl.BlockSpec((1, rows, width), lambda b, j: (b, j, 0))

    return pl.pallas_call(
        functools.partial(_ret_mix_kernel, n_chunks=n_chunks),
        grid=(B, S // rows),
        in_specs=[spec(qk_w), spec(qk_w), spec(RET_WIDTH), spec(RET_WIDTH),
                  _full_spec((1, RET_WIDTH))],
        out_specs=spec(RET_WIDTH),
        out_shape=jax.ShapeDtypeStruct((B, S, RET_WIDTH), BF16),
        scratch_shapes=[pltpu.VMEM((RET_HEADS, RET_QK, RET_V), F32),
                        pltpu.VMEM((RET_HEADS, RET_CHUNK, RET_CHUNK), F32),
                        pltpu.VMEM((RET_HEADS, RET_CHUNK, RET_V), F32),
                        pltpu.VMEM((RET_HEADS, RET_CHUNK, RET_QK), F32)],
        compiler_params=_params("arbitrary", "arbitrary"),
        name="ret_mix",
    )(q, k, v, sg, gn)


def _mla_in_kernel(x_ref, g_ref, w_ref, qg_ref, kvg_ref, tab_ref, wuk_ref, wuv_ref,
                   cq_ref, k_ref, v_ref, sg_ref):
    h = _rms(x_ref[...], g_ref[...]).astype(BF16)
    o1 = Q_LORA
    o2 = o1 + KV_LORA
    o3 = o2 + LANES
    q_scale = (MLA_NOPE + MLA_ROPE) ** -0.5 * math.log2(math.e)
    cq_ref[...] = _rms(_dot(h, w_ref[:, :o1]), qg_ref[...] * q_scale).astype(BF16)
    c_kv = _rms(_dot(h, w_ref[:, o1:o2]), kvg_ref[...]).astype(BF16)
    t = _dot(h, w_ref[:, o2:o3]) * tab_ref[...]
    t = t + pltpu.roll(t, MLA_ROPE, 1)
    lane = lax.broadcasted_iota(jnp.int32, t.shape, 1)
    k_tail = jnp.where(lane < MLA_ROPE, t, 0.0).astype(BF16)
    k_nope = _dot(c_kv, wuk_ref[...]).astype(BF16)
    for hd in range(MLA_HEADS):
        c0 = hd * MLA_QK_PAD
        k_ref[:, c0:c0 + MLA_NOPE] = k_nope[:, hd * MLA_NOPE:(hd + 1) * MLA_NOPE]
        k_ref[:, c0 + MLA_NOPE:c0 + MLA_QK_PAD] = k_tail
    vt = lax.dot_general(wuv_ref[...], c_kv, _NT, preferred_element_type=F32).astype(BF16)
    ones = jnp.ones((MLA_V_AUG - MLA_V, vt.shape[1]), BF16)
    for hd in range(MLA_HEADS):
        v_ref[0, hd, :MLA_V, :] = vt[hd * MLA_V:(hd + 1) * MLA_V]
        v_ref[0, hd, MLA_V:, :] = ones
    chunk = 512
    for c in range(MLA_WIDTH // chunk):
        gate = _dot(h, w_ref[:, o3 + c * chunk:o3 + (c + 1) * chunk])
        sg_ref[:, c * chunk:(c + 1) * chunk] = _silu(gate).astype(BF16)


def _mla_in(x, g, w, qg, kvg, tab, wuk, wuv, tm):
    T = x.shape[0]
    kw = MLA_HEADS * MLA_QK_PAD
    return pl.pallas_call(
        _mla_in_kernel,
        grid=(T // tm,),
        in_specs=[_row_spec(tm, D_MODEL), _full_spec((1, D_MODEL)), _full_spec(w.shape),
                  _full_spec((1, Q_LORA)), _full_spec((1, KV_LORA)), _row_spec(tm, LANES),
                  _full_spec(wuk.shape), _full_spec(wuv.shape)],
        out_specs=[_row_spec(tm, Q_LORA), _row_spec(tm, kw),
                   pl.BlockSpec((1, MLA_HEADS, MLA_V_AUG, tm), lambda i: (i, 0, 0, 0)),
                   _row_spec(tm, MLA_WIDTH)],
        out_shape=[jax.ShapeDtypeStruct((T, Q_LORA), BF16), jax.ShapeDtypeStruct((T, kw), BF16),
                   jax.ShapeDtypeStruct((T // tm, MLA_HEADS, MLA_V_AUG, tm), BF16),
                   jax.ShapeDtypeStruct((T, MLA_WIDTH), BF16)],
        compiler_params=_params("parallel"),
        name="mla_in",
    )(x, g, w, qg, kvg, tab, wuk, wuv)


def _mla_q_kernel(cq_ref, w_ref, tabt_ref, q_ref):
    cq = cq_ref[...]
    half = MLA_ROPE // 2
    cos = tabt_ref[:half, :]
    sin = tabt_ref[MLA_ROPE:MLA_ROPE + half, :]
    group = 4
    per_head = MLA_NOPE + MLA_ROPE
    rows = group * per_head
    for g in range(MLA_HEADS // group):
        z = lax.dot_general(w_ref[g * rows:(g + 1) * rows, :], cq, _NT,
                            preferred_element_type=F32)
        for r in range(group):
            hd = g * group + r
            r0 = r * per_head
            q_ref[0, hd, :MLA_NOPE, :] = z[r0:r0 + MLA_NOPE].astype(BF16)
            x1 = z[r0 + MLA_NOPE:r0 + MLA_NOPE + half]
            x2 = z[r0 + MLA_NOPE + half:r0 + per_head]
            q_ref[0, hd, MLA_NOPE:MLA_NOPE + half, :] = (x1 * cos - x2 * sin).astype(BF16)
            q_ref[0, hd, MLA_NOPE + half:per_head, :] = (x2 * cos + x1 * sin).astype(BF16)
            q_ref[0, hd, per_head:, :] = jnp.zeros((MLA_QK_PAD - per_head, z.shape[1]), BF16)


def _mla_q(cq, wt, tabt, tq):
    T = cq.shape[0]
    return pl.pallas_call(
        _mla_q_kernel,
        grid=(T // tq,),
        in_specs=[_row_spec(tq, Q_LORA), _full_spec(wt.shape),
                  pl.BlockSpec((LANES, tq), lambda i: (0, i))],
        out_specs=pl.BlockSpec((1, MLA_HEADS, MLA_QK_PAD, tq), lambda i: (i, 0, 0, 0)),
        out_shape=jax.ShapeDtypeStruct((T // tq, MLA_HEADS, MLA_QK_PAD, tq), BF16),
        compiler_params=_params("parallel"),
        name="mla_q",
    )(cq, wt, tabt)


def _attn_kernel(qt_ref, k_ref, vt_ref, sg_ref, o_ref, acc_ref, s0_ref, s1_ref,
                 p0_ref, p1_ref, *, tq, hp):
    i = pl.program_id(2)
    tk = tq // 2
    acc_ref[...] = jnp.zeros_like(acc_ref)

    def stage_a(a, key0, s_ref):
        k = k_ref[0, pl.ds(pl.multiple_of(key0, tk), tk), a * MLA_QK_PAD:(a + 1) * MLA_QK_PAD]
        s = _dot(k, qt_ref[0, a])
        s_ref[a] = s
        return jnp.max(s, axis=0, keepdims=True)

    def stage_b(a, s_ref, p_ref, m, cm):
        m_new = jnp.maximum(m, cm)
        p_ref[a] = jnp.exp2((s_ref[a] - m_new).astype(BF16))
        return m_new, jnp.exp2(m - m_new)

    def stage_c(a, p_ref, vt, alpha):
        acc_ref[a] = alpha * acc_ref[a] + _dot(vt, p_ref[a])

    tri = (lax.broadcasted_iota(jnp.int32, (tk, tk), 0)
           <= lax.broadcasted_iota(jnp.int32, (tk, tk), 1))
    state = []
    for a in range(hp):
        k_hi = k_ref[0, pl.ds(pl.multiple_of(i * tq + tk, tk), tk),
                     a * MLA_QK_PAD:(a + 1) * MLA_QK_PAD]
        s_hi = jnp.where(tri, _dot(k_hi, qt_ref[0, a, :, tk:]), -1e30)
        cm_hi = jnp.max(s_hi, axis=0, keepdims=True)
        p0_ref[a, :, :tk] = jnp.zeros((tk, tk), BF16)
        p0_ref[a, :, tk:] = jnp.exp2((s_hi - cm_hi).astype(BF16))
        m = jnp.concatenate([jnp.full((1, tk), -jnp.inf, F32), cm_hi], axis=1)
        k_lo = k_ref[0, pl.ds(pl.multiple_of(i * tq, tk), tk),
                     a * MLA_QK_PAD:(a + 1) * MLA_QK_PAD]
        s_lo = _dot(k_lo, qt_ref[0, a])
        s_tri = jnp.where(tri, s_lo[:, :tk], -1e30)
        s1_ref[a, :, :tk] = s_tri
        s1_ref[a, :, tk:] = s_lo[:, tk:]
        cm1 = jnp.concatenate([jnp.max(s_tri, axis=0, keepdims=True),
                               jnp.max(s_lo[:, tk:], axis=0, keepdims=True)], axis=1)
        state.append((m, cm1, jnp.zeros((1, tq), F32)))

    def pair(jj, state):
        blk = jnp.where(jj == 0, 2 * i + 1, 2 * (jj - 1))
        blk_next = jnp.where(jj == 0, 2 * i, 2 * (jj - 1) + 1)
        heads = range(hp)
        ms = [st[0] for st in state]
        cm1s = [st[1] for st in state]
        alpha0s = [st[2] for st in state]
        alpha1s = [None] * hp
        cm0s = [stage_a(a, jj * tq, s0_ref) for a in heads]
        for a in heads:
            stage_c(a, p0_ref, vt_ref[blk, a], alpha0s[a])
        for a in heads:
            ms[a], alpha1s[a] = stage_b(a, s1_ref, p1_ref, ms[a], cm1s[a])
        cm1s = [stage_a(a, jj * tq + tk, s1_ref) for a in heads]
        for a in heads:
            stage_c(a, p1_ref, vt_ref[blk_next, a], alpha1s[a])
        for a in heads:
            ms[a], alpha0s[a] = stage_b(a, s0_ref, p0_ref, ms[a], cm0s[a])
        return tuple(zip(ms, cm1s, alpha0s))

    state = lax.fori_loop(0, i, pair, tuple(state))
    blk = jnp.where(i == 0, 1, 2 * (i - 1))
    blk_next = jnp.where(i == 0, 0, 2 * (i - 1) + 1)
    for a in range(hp):
        m, cm1, alpha0 = state[a]
        _, alpha1 = stage_b(a, s1_ref, p1_ref, m, cm1)
        stage_c(a, p0_ref, vt_ref[blk, a], alpha0)
        stage_c(a, p1_ref, vt_ref[blk_next, a], alpha1)
        acc = acc_ref[a]
        o = (acc[:MLA_V] / acc[MLA_V:MLA_V + 1]).T
        cs = slice(a * MLA_V, (a + 1) * MLA_V)
        o_ref[0, :, cs] = (o * sg_ref[0, :, cs].astype(F32)).astype(BF16)


def _attn(qt, k, vt, sg, tq, hp):
    B, S, _ = k.shape
    nq = S // tq
    tk = tq // 2
    return pl.pallas_call(
        functools.partial(_attn_kernel, tq=tq, hp=hp),
        grid=(B, MLA_HEADS // hp, nq),
        in_specs=[pl.BlockSpec((1, hp, MLA_QK_PAD, tq), lambda b, h, i: (b * nq + i, h, 0, 0)),
                  pl.BlockSpec((1, S, hp * MLA_QK_PAD), lambda b, h, i: (b, 0, h)),
                  pl.BlockSpec((S // tk, hp, MLA_V_AUG, tk), lambda b, h, i: (b, h, 0, 0)),
                  pl.BlockSpec((1, tq, hp * MLA_V), lambda b, h, i: (b, i, h))],
        out_specs=pl.BlockSpec((1, tq, hp * MLA_V), lambda b, h, i: (b, i, h)),
        out_shape=jax.ShapeDtypeStruct((B, S, MLA_WIDTH), BF16),
        scratch_shapes=[pltpu.VMEM((hp, MLA_V_AUG, tq), F32),
                        pltpu.VMEM((hp, tk, tq), F32), pltpu.VMEM((hp, tk, tq), F32),
                        pltpu.VMEM((hp, tk, tq), BF16), pltpu.VMEM((hp, tk, tq), BF16)],
        compiler_params=_params("parallel", "parallel", "arbitrary"),
        name="mla_attn",
    )(qt, k, vt, sg)


def _out_kernel(og_ref, x_ref, p_ref, wo_ref, pg_ref, wp_ref, wg_ref, out_ref):
    y = _dot(og_ref[...], wo_ref[...])
    x1 = x_ref[...] + _rms(y, pg_ref[...])
    pe = _dot(p_ref[...].astype(BF16), wp_ref[...])
    gt = jax.nn.sigmoid(_dot(x1.astype(BF16), wg_ref[...]))
    out_ref[...] = x1 + pe * gt


def _out(og, x, p, layer, wo, pg, wp, wg, tm):
    T = x.shape[0]
    width = og.shape[1]
    return pl.pallas_call(
        _out_kernel,
        grid=(T // tm,),
        in_specs=[_row_spec(tm, width), _row_spec(tm, D_MODEL),
                  pl.BlockSpec((None, tm, PLE_DIM), lambda i: (layer, i, 0)),
                  _full_spec(wo.shape), _full_spec((1, D_MODEL)),
                  _full_spec(wp.shape), _full_spec(wg.shape)],
        out_specs=_row_spec(tm, D_MODEL),
        out_shape=jax.ShapeDtypeStruct((T, D_MODEL), F32),
        compiler_params=_params("parallel"),
        name="branch_out",
    )(og, x, p, wo, pg, wp, wg)


def _rotate_half_cols(w):
    half = w.shape[-1] // 2
    return jnp.concatenate([-w[..., half:], w[..., :half]], axis=-1)


def _prep_mla_w_in(w):
    o2 = Q_LORA + KV_LORA
    o3 = o2 + MLA_ROPE
    kr = w[:, o2:o3]
    return jnp.concatenate([w[:, :o3], _rotate_half_cols(kr), w[:, o3:]], axis=1).astype(BF16)


def kernel(x, p, positions, pre_norm_g, post_norm_g, ret_w_in, ret_gn_g, ret_w_out, mla_w_in,
           mla_q_norm_g, mla_kv_norm_g, mla_w_uq, mla_w_uk, mla_w_uv, mla_w_out,
           ple_w_proj, ple_w_gate):
    B, S, D = x.shape
    T = B * S
    tm = min(512, T)
    tq = 2 * tm
    assert S % tq == 0
    n_chunks = min(2, S // RET_CHUNK)

    cos, sin, tab, tabt = _rope_tables(positions)
    xf = x.reshape(T, D)
    pf = p.reshape(DEPTH, T, PLE_DIM)

    for i in range(DEPTH):
        j = i // 2
        g_pre = pre_norm_g[i].reshape(1, D)
        if i % 2 == 0:
            q, k, v, sg = _ret_in(xf, g_pre, ret_w_in[j].astype(BF16), cos, sin, tm)
            og = _ret_mix(q.reshape(B, S, -1), k.reshape(B, S, -1), v.reshape(B, S, -1),
                          sg.reshape(B, S, -1), ret_gn_g[j].reshape(1, RET_WIDTH), n_chunks)
            w_out = ret_w_out[j]
        else:
            wuk = mla_w_uk[j].reshape(KV_LORA, MLA_HEADS * MLA_NOPE).astype(BF16)
            wuv_t = mla_w_uv[j].reshape(KV_LORA, MLA_WIDTH).T.astype(BF16)
            cq, k, vt, sg = _mla_in(xf, g_pre, _prep_mla_w_in(mla_w_in[j]),
                                    mla_q_norm_g[j].reshape(1, Q_LORA),
                                    mla_kv_norm_g[j].reshape(1, KV_LORA), tab, wuk, wuv_t, tm)
            qt = _mla_q(cq, mla_w_uq[j].T.astype(BF16), tabt, tq)
            og = _attn(qt, k.reshape(B, S, -1), vt, sg.reshape(B, S, -1), tq, ATTN_HEADS_PER_STEP)
            w_out = mla_w_out[j]
        xf = _out(og.reshape(T, -1), xf, pf, i, w_out.astype(BF16),
                  post_norm_g[i].reshape(1, D), ple_w_proj[i].astype(BF16),
                  ple_w_gate[i].astype(BF16), tm)
    return xf.reshape(B, S, D)
```

```python
import functools
import math

import jax
import jax.numpy as jnp
from jax import lax
from jax.experimental import pallas as pl
from jax.experimental.pallas import tpu as pltpu

D_MODEL = 1024
DEPTH = 4
ROPE_BASE = 10000.0
NORM_EPS = 1e-6

RET_HEADS = 4
RET_QK = 256
RET_V = 512
RET_WIDTH = RET_HEADS * RET_V
RET_CHUNK = 256
RET_LOG_G = tuple(math.log(1.0 - 2.0 ** (-5.0 - h)) for h in range(RET_HEADS))

MLA_HEADS = 16
MLA_NOPE = 128
MLA_ROPE = 64
MLA_V = 128
Q_LORA = 768
KV_LORA = 256
MLA_WIDTH = MLA_HEADS * MLA_V
MLA_QK_PAD = 256
MLA_V_AUG = MLA_V + 16
ATTN_HEADS_PER_STEP = 2
PLE_DIM = 256

LANES = 128
VMEM_LIMIT = 56 * 1024 * 1024

BF16 = jnp.bfloat16
F32 = jnp.float32

_NT = (((1,), (1,)), ((), ()))
_TN = (((0,), (0,)), ((), ()))


def _dot(a, b):
    return jnp.dot(a, b, preferred_element_type=F32)


def _rms(x, g):
    return x * lax.rsqrt(jnp.mean(x * x, axis=-1, keepdims=True) + NORM_EPS) * g


def _silu(x):
    return x * jax.nn.sigmoid(x)


def _params(*sem):
    return pltpu.CompilerParams(dimension_semantics=sem, vmem_limit_bytes=VMEM_LIMIT)


def _row_spec(tm, width):
    return pl.BlockSpec((tm, width), lambda i: (i, 0))


def _full_spec(shape):
    return pl.BlockSpec(shape, lambda *_: (0,) * len(shape))


def _rope_table_kernel(pos_ref, posr_ref, fret_ref, fmlac_ref,
                       cos_ref, sin_ref, tab_ref, tabt_ref):
    pos = pos_ref[...].astype(F32)
    ang = pos * fret_ref[...]
    cos_ref[...] = jnp.cos(ang)
    sin_ref[...] = jnp.sin(ang)
    ang2 = fmlac_ref[...] * posr_ref[...].astype(F32)
    c = jnp.cos(ang2)
    s = jnp.sin(ang2)
    tabt = jnp.concatenate([c, c, s, s], axis=0)
    tabt_ref[...] = tabt
    tab_ref[...] = tabt.T


def _rope_tables(positions):
    T = positions.size
    ts = min(1024, T)
    half_r = RET_QK // 2
    half_m = MLA_ROPE // 2
    f_ret = (ROPE_BASE ** (-jnp.arange(half_r, dtype=F32) / half_r)).reshape(1, half_r)
    f_mla = (ROPE_BASE ** (-jnp.arange(half_m, dtype=F32) / half_m)).reshape(half_m, 1)
    out = jax.ShapeDtypeStruct((T, LANES), F32)
    return pl.pallas_call(
        _rope_table_kernel,
        grid=(T // ts,),
        in_specs=[_row_spec(ts, 1), pl.BlockSpec((1, ts), lambda i: (0, i)),
                  _full_spec((1, LANES)), _full_spec((half_m, 1))],
        out_specs=[_row_spec(ts, LANES)] * 3 + [pl.BlockSpec((LANES, ts), lambda i: (0, i))],
        out_shape=[out, out, out, jax.ShapeDtypeStruct((LANES, T), F32)],
        compiler_params=_params("parallel"),
        name="rope_tables",
    )(positions.reshape(T, 1), positions.reshape(1, T), f_ret, f_mla)


def _ret_in_kernel(x_ref, g_ref, w_ref, cos_ref, sin_ref, q_ref, k_ref, v_ref, sg_ref):
    h = _rms(x_ref[...], g_ref[...]).astype(BF16)
    cos = cos_ref[...]
    sin = sin_ref[...]
    half = RET_QK // 2
    k_scale = RET_QK ** -0.5
    for j in range(2 * RET_HEADS):
        z = _dot(h, w_ref[:, j * RET_QK:(j + 1) * RET_QK])
        x1 = z[:, :half]
        x2 = z[:, half:]
        r1 = x1 * cos - x2 * sin
        r2 = x2 * cos + x1 * sin
        if j < RET_HEADS:
            q_ref[:, j * RET_QK:j * RET_QK + half] = r1.astype(BF16)
            q_ref[:, j * RET_QK + half:(j + 1) * RET_QK] = r2.astype(BF16)
        else:
            c0 = (j - RET_HEADS) * RET_QK
            k_ref[:, c0:c0 + half] = (r1 * k_scale).astype(BF16)
            k_ref[:, c0 + half:c0 + RET_QK] = (r2 * k_scale).astype(BF16)
    v0 = 2 * RET_HEADS * RET_QK
    g0 = v0 + RET_WIDTH
    for c in range(RET_HEADS):
        cs = slice(c * RET_V, (c + 1) * RET_V)
        v_ref[:, cs] = _dot(h, w_ref[:, v0 + c * RET_V:v0 + (c + 1) * RET_V]).astype(BF16)
        gate = _dot(h, w_ref[:, g0 + c * RET_V:g0 + (c + 1) * RET_V])
        sg_ref[:, cs] = _silu(gate).astype(BF16)


def _ret_in(x, g, w, cos, sin, tm):
    T = x.shape[0]
    qk_w = RET_HEADS * RET_QK
    return pl.pallas_call(
        _ret_in_kernel,
        grid=(T // tm,),
        in_specs=[_row_spec(tm, D_MODEL), _full_spec((1, D_MODEL)), _full_spec(w.shape),
                  _row_spec(tm, LANES), _row_spec(tm, LANES)],
        out_specs=[_row_spec(tm, qk_w), _row_spec(tm, qk_w),
                   _row_spec(tm, RET_WIDTH), _row_spec(tm, RET_WIDTH)],
        out_shape=[jax.ShapeDtypeStruct((T, qk_w), BF16), jax.ShapeDtypeStruct((T, qk_w), BF16),
                   jax.ShapeDtypeStruct((T, RET_WIDTH), BF16),
                   jax.ShapeDtypeStruct((T, RET_WIDTH), BF16)],
        compiler_params=_params("parallel"),
        name="ret_in",
    )(x, g, w, cos, sin)


def _ret_mix_kernel(q_ref, k_ref, v_ref, sg_ref, gn_ref, o_ref,
                    state_ref, dmask_ref, qdec_ref, kdec_ref, *, n_chunks):
    C = RET_CHUNK

    @pl.when(pl.program_id(1) == 0)
    def _init():
        state_ref[...] = jnp.zeros_like(state_ref)
        row = lax.broadcasted_iota(jnp.int32, (C, C), 0)
        col = lax.broadcasted_iota(jnp.int32, (C, C), 1)
        rel = (row - col).astype(F32)
        iq = lax.broadcasted_iota(jnp.int32, (C, RET_V), 0).astype(F32)
        ik = lax.broadcasted_iota(jnp.int32, (C, RET_QK), 0).astype(F32)
        for h in range(RET_HEADS):
            lg = RET_LOG_G[h]
            dmask_ref[h] = jnp.where(rel >= 0, jnp.exp(jnp.maximum(rel, 0.0) * lg), 0.0)
            qdec_ref[h] = jnp.exp((iq + 1.0) * lg)
            kdec_ref[h] = jnp.exp((C - 1.0 - ik) * lg)

    for c in range(n_chunks):
        rows = slice(c * C, (c + 1) * C)
        for h in range(RET_HEADS):
            qs = slice(h * RET_QK, (h + 1) * RET_QK)
            vs = slice(h * RET_V, (h + 1) * RET_V)
            qh = q_ref[0, rows, qs]
            kh = k_ref[0, rows, qs]
            vh = v_ref[0, rows, vs]
            a = lax.dot_general(qh, kh, _NT, preferred_element_type=F32) * dmask_ref[h]
            st = state_ref[h]
            o = _dot(a.astype(BF16), vh) + qdec_ref[h] * _dot(qh, st.astype(BF16))
            kd = (kh.astype(F32) * kdec_ref[h]).astype(BF16)
            state_ref[h] = st * math.exp(C * RET_LOG_G[h]) + lax.dot_general(
                kd, vh, _TN, preferred_element_type=F32)
            o = o * lax.rsqrt(jnp.mean(o * o, axis=-1, keepdims=True) + NORM_EPS)
            o = o * gn_ref[:, vs] * sg_ref[0, rows, vs].astype(F32)
            o_ref[0, rows, vs] = o.astype(BF16)


def _ret_mix(q, k, v, sg, gn, n_chunks):
    B, S, _ = q.shape
    rows = n_chunks * RET_CHUNK
    qk_w = RET_HEADS * RET_QK

    def spec(width):
        return pl.BlockSpec((1, rows, width), lambda b, j: (b, j, 0))

    return pl.pallas_call(
        functools.partial(_ret_mix_kernel, n_chunks=n_chunks),
        grid=(B, S // rows),
        in_specs=[spec(qk_w), spec(qk_w), spec(RET_WIDTH), spec(RET_WIDTH),
                  _full_spec((1, RET_WIDTH))],
        out_specs=spec(RET_WIDTH),
        out_shape=jax.ShapeDtypeStruct((B, S, RET_WIDTH), BF16),
        scratch_shapes=[pltpu.VMEM((RET_HEADS, RET_QK, RET_V), F32),
                        pltpu.VMEM((RET_HEADS, RET_CHUNK, RET_CHUNK), F32),
                        pltpu.VMEM((RET_HEADS, RET_CHUNK, RET_V), F32),
                        pltpu.VMEM((RET_HEADS, RET_CHUNK, RET_QK), F32)],
        compiler_params=_params("arbitrary", "arbitrary"),
        name="ret_mix",
    )(q, k, v, sg, gn)


def _mla_in_kernel(x_ref, g_ref, w_ref, qg_ref, kvg_ref, tab_ref, wuk_ref, wuv_ref,
                   cq_ref, k_ref, v_ref, sg_ref):
    h = _rms(x_ref[...], g_ref[...]).astype(BF16)
    o1 = Q_LORA
    o2 = o1 + KV_LORA
    o3 = o2 + LANES
    q_scale = (MLA_NOPE + MLA_ROPE) ** -0.5 * math.log2(math.e)
    cq_ref[...] = _rms(_dot(h, w_ref[:, :o1]), qg_ref[...] * q_scale).astype(BF16)
    c_kv = _rms(_dot(h, w_ref[:, o1:o2]), kvg_ref[...]).astype(BF16)
    t = _dot(h, w_ref[:, o2:o3]) * tab_ref[...]
    t = t + pltpu.roll(t, MLA_ROPE, 1)
    lane = lax.broadcasted_iota(jnp.int32, t.shape, 1)
    k_tail = jnp.where(lane < MLA_ROPE, t, 0.0).astype(BF16)
    k_nope = _dot(c_kv, wuk_ref[...]).astype(BF16)
    for hd in range(MLA_HEADS):
        c0 = hd * MLA_QK_PAD
        k_ref[:, c0:c0 + MLA_NOPE] = k_nope[:, hd * MLA_NOPE:(hd + 1) * MLA_NOPE]
        k_ref[:, c0 + MLA_NOPE:c0 + MLA_QK_PAD] = k_tail
    vt = lax.dot_general(wuv_ref[...], c_kv, _NT, preferred_element_type=F32).astype(BF16)
    ones = jnp.ones((MLA_V_AUG - MLA_V, vt.shape[1]), BF16)
    for hd in range(MLA_HEADS):
        v_ref[0, hd, :MLA_V, :] = vt[hd * MLA_V:(hd + 1) * MLA_V]
        v_ref[0, hd, MLA_V:, :] = ones
    chunk = 512
    for c in range(MLA_WIDTH // chunk):
        gate = _dot(h, w_ref[:, o3 + c * chunk:o3 + (c + 1) * chunk])
        sg_ref[:, c * chunk:(c + 1) * chunk] = _silu(gate).astype(BF16)


def _mla_in(x, g, w, qg, kvg, tab, wuk, wuv, tm):
    T = x.shape[0]
    kw = MLA_HEADS * MLA_QK_PAD
    return pl.pallas_call(
        _mla_in_kernel,
        grid=(T // tm,),
        in_specs=[_row_spec(tm, D_MODEL), _full_spec((1, D_MODEL)), _full_spec(w.shape),
                  _full_spec((1, Q_LORA)), _full_spec((1, KV_LORA)), _row_spec(tm, LANES),
                  _full_spec(wuk.shape), _full_spec(wuv.shape)],
        out_specs=[_row_spec(tm, Q_LORA), _row_spec(tm, kw),
                   pl.BlockSpec((1, MLA_HEADS, MLA_V_AUG, tm), lambda i: (i, 0, 0, 0)),
                   _row_spec(tm, MLA_WIDTH)],
        out_shape=[jax.ShapeDtypeStruct((T, Q_LORA), BF16), jax.ShapeDtypeStruct((T, kw), BF16),
                   jax.ShapeDtypeStruct((T // tm, MLA_HEADS, MLA_V_AUG, tm), BF16),
                   jax.ShapeDtypeStruct((T, MLA_WIDTH), BF16)],
        compiler_params=_params("parallel"),
        name="mla_in",
    )(x, g, w, qg, kvg, tab, wuk, wuv)


def _mla_q_kernel(cq_ref, w_ref, tabt_ref, q_ref):
    cq = cq_ref[...]
    half = MLA_ROPE // 2
    cos = tabt_ref[:half, :]
    sin = tabt_ref[MLA_ROPE:MLA_ROPE + half, :]
    group = 4
    per_head = MLA_NOPE + MLA_ROPE
    rows = group * per_head
    for g in range(MLA_HEADS // group):
        z = lax.dot_general(w_ref[g * rows:(g + 1) * rows, :], cq, _NT,
                            preferred_element_type=F32)
        for r in range(group):
            hd = g * group + r
            r0 = r * per_head
            q_ref[0, hd, :MLA_NOPE, :] = z[r0:r0 + MLA_NOPE].astype(BF16)
            x1 = z[r0 + MLA_NOPE:r0 + MLA_NOPE + half]
            x2 = z[r0 + MLA_NOPE + half:r0 + per_head]
            q_ref[0, hd, MLA_NOPE:MLA_NOPE + half, :] = (x1 * cos - x2 * sin).astype(BF16)
            q_ref[0, hd, MLA_NOPE + half:per_head, :] = (x2 * cos + x1 * sin).astype(BF16)
            q_ref[0, hd, per_head:, :] = jnp.zeros((MLA_QK_PAD - per_head, z.shape[1]), BF16)


def _mla_q(cq, wt, tabt, tq):
    T = cq.shape[0]
    return pl.pallas_call(
        _mla_q_kernel,
        grid=(T // tq,),
        in_specs=[_row_spec(tq, Q_LORA), _full_spec(wt.shape),
                  pl.BlockSpec((LANES, tq), lambda i: (0, i))],
        out_specs=pl.BlockSpec((1, MLA_HEADS, MLA_QK_PAD, tq), lambda i: (i, 0, 0, 0)),
        out_shape=jax.ShapeDtypeStruct((T // tq, MLA_HEADS, MLA_QK_PAD, tq), BF16),
        compiler_params=_params("parallel"),
        name="mla_q",
    )(cq, wt, tabt)


def _attn_kernel(qt_ref, k_ref, vt_ref, sg_ref, o_ref, acc_ref, s0_ref, s1_ref,
                 p0_ref, p1_ref, *, tq, hp):
    i = pl.program_id(2)
    tk = tq // 2

    def stage_a(a, key0, s_ref):
        k = k_ref[0, pl.ds(pl.multiple_of(key0, tk), tk), a * MLA_QK_PAD:(a + 1) * MLA_QK_PAD]
        s = _dot(k, qt_ref[0, a])
        s_ref[a] = s
        return jnp.max(s, axis=0, keepdims=True)

    def stage_b(a, s_ref, p_ref, m, cm):
        m_new = jnp.maximum(m, cm)
        p_ref[a] = jnp.exp2((s_ref[a] - m_new).astype(BF16))
        return m_new, jnp.exp2(m - m_new)

    def stage_c(a, p_ref, vt, alpha):
        acc_ref[a] = alpha * acc_ref[a] + _dot(vt, p_ref[a])

    tri = (lax.broadcasted_iota(jnp.int32, (tk, tk), 0)
           <= lax.broadcasted_iota(jnp.int32, (tk, tk), 1))
    state = []
    for a in range(hp):
        k_hi = k_ref[0, pl.ds(pl.multiple_of(i * tq + tk, tk), tk),
                     a * MLA_QK_PAD:(a + 1) * MLA_QK_PAD]
        s_hi = jnp.where(tri, _dot(k_hi, qt_ref[0, a, :, tk:]), -1e30)
        cm_hi = jnp.max(s_hi, axis=0, keepdims=True)
        p0_ref[a, :, :tk] = jnp.zeros((tk, tk), BF16)
        p0_ref[a, :, tk:] = jnp.exp2((s_hi - cm_hi).astype(BF16))
        m = jnp.concatenate([jnp.full((1, tk), -jnp.inf, F32), cm_hi], axis=1)
        k_lo = k_ref[0, pl.ds(pl.multiple_of(i * tq, tk), tk),
                     a * MLA_QK_PAD:(a + 1) * MLA_QK_PAD]
        s_lo = _dot(k_lo, qt_ref[0, a])
        s_tri = jnp.where(tri, s_lo[:, :tk], -1e30)
        s1_ref[a, :, :tk] = s_tri
        s1_ref[a, :, tk:] = s_lo[:, tk:]
        cm1 = jnp.concatenate([jnp.max(s_tri, axis=0, keepdims=True),
                               jnp.max(s_lo[:, tk:], axis=0, keepdims=True)], axis=1)
        state.append((m, cm1, jnp.zeros((1, tq), F32)))
    acc_ref[...] = jnp.zeros_like(acc_ref)

    def pair(jj, state):
        blk = jnp.where(jj == 0, 2 * i + 1, 2 * (jj - 1))
        blk_next = jnp.where(jj == 0, 2 * i, 2 * (jj - 1) + 1)
        heads = range(hp)
        ms = [st[0] for st in state]
        cm1s = [st[1] for st in state]
        alpha0s = [st[2] for st in state]
        alpha1s = [None] * hp
        cm0s = [stage_a(a, jj * tq, s0_ref) for a in heads]
        for a in heads:
            stage_c(a, p0_ref, vt_ref[blk, a], alpha0s[a])
        for a in heads:
            ms[a], alpha1s[a] = stage_b(a, s1_ref, p1_ref, ms[a], cm1s[a])
        cm1s = [stage_a(a, jj * tq + tk, s1_ref) for a in heads]
        for a in heads:
            stage_c(a, p1_ref, vt_ref[blk_next, a], alpha1s[a])
        for a in heads:
            ms[a], alpha0s[a] = stage_b(a, s0_ref, p0_ref, ms[a], cm0s[a])
        return tuple(zip(ms, cm1s, alpha0s))

    state = lax.fori_loop(0, i, pair, tuple(state))
    blk = jnp.where(i == 0, 1, 2 * (i - 1))
    blk_next = jnp.where(i == 0, 0, 2 * (i - 1) + 1)
    for a in range(hp):
        m, cm1, alpha0 = state[a]
        _, alpha1 = stage_b(a, s1_ref, p1_ref, m, cm1)
        stage_c(a, p0_ref, vt_ref[blk, a], alpha0)
        stage_c(a, p1_ref, vt_ref[blk_next, a], alpha1)
        acc = acc_ref[a]
        o = (acc[:MLA_V] / acc[MLA_V:MLA_V + 1]).T
        cs = slice(a * MLA_V, (a + 1) * MLA_V)
        o_ref[0, :, cs] = (o * sg_ref[0, :, cs].astype(F32)).astype(BF16)


def _attn(qt, k, vt, sg, tq, hp):
    B, S, _ = k.shape
    nq = S // tq
    tk = tq // 2
    return pl.pallas_call(
        functools.partial(_attn_kernel, tq=tq, hp=hp),
        grid=(B, MLA_HEADS // hp, nq),
        in_specs=[pl.BlockSpec((1, hp, MLA_QK_PAD, tq), lambda b, h, i: (b * nq + i, h, 0, 0)),
                  pl.BlockSpec((1, S, hp * MLA_QK_PAD), lambda b, h, i: (b, 0, h)),
                  pl.BlockSpec((S // tk, hp, MLA_V_AUG, tk), lambda b, h, i: (b, h, 0, 0)),
                  pl.BlockSpec((1, tq, hp * MLA_V), lambda b, h, i: (b, i, h))],
        out_specs=pl.BlockSpec((1, tq, hp * MLA_V), lambda b, h, i: (b, i, h)),
        out_shape=jax.ShapeDtypeStruct((B, S, MLA_WIDTH), BF16),
        scratch_shapes=[pltpu.VMEM((hp, MLA_V_AUG, tq), F32),
                        pltpu.VMEM((hp, tk, tq), F32), pltpu.VMEM((hp, tk, tq), F32),
                        pltpu.VMEM((hp, tk, tq), BF16), pltpu.VMEM((hp, tk, tq), BF16)],
        compiler_params=_params("parallel", "parallel", "arbitrary"),
        name="mla_attn",
    )(qt, k, vt, sg)


def _out_kernel(og_ref, x_ref, p_ref, wo_ref, pg_ref, wp_ref, wg_ref, out_ref):
    y = _dot(og_ref[...], wo_ref[...])
    x1 = x_ref[...] + _rms(y, pg_ref[...])
    pe = _dot(p_ref[...].astype(BF16), wp_ref[...])
    gt = jax.nn.sigmoid(_dot(x1.astype(BF16), wg_ref[...]))
    out_ref[...] = x1 + pe * gt


def _out(og, x, p, layer, wo, pg, wp, wg, tm):
    T = x.shape[0]
    width = og.shape[1]
    return pl.pallas_call(
        _out_kernel,
        grid=(T // tm,),
        in_specs=[_row_spec(tm, width), _row_spec(tm, D_MODEL),
                  pl.BlockSpec((None, tm, PLE_DIM), lambda i: (layer, i, 0)),
                  _full_spec(wo.shape), _full_spec((1, D_MODEL)),
                  _full_spec(wp.shape), _full_spec(wg.shape)],
        out_specs=_row_spec(tm, D_MODEL),
        out_shape=jax.ShapeDtypeStruct((T, D_MODEL), F32),
        compiler_params=_params("parallel"),
        name="branch_out",
    )(og, x, p, wo, pg, wp, wg)


def _rotate_half_cols(w):
    half = w.shape[-1] // 2
    return jnp.concatenate([-w[..., half:], w[..., :half]], axis=-1)


def _prep_mla_w_in(w):
    o2 = Q_LORA + KV_LORA
    o3 = o2 + MLA_ROPE
    kr = w[:, o2:o3]
    return jnp.concatenate([w[:, :o3], _rotate_half_cols(kr), w[:, o3:]], axis=1).astype(BF16)


def kernel(x, p, positions, pre_norm_g, post_norm_g, ret_w_in, ret_gn_g, ret_w_out, mla_w_in,
           mla_q_norm_g, mla_kv_norm_g, mla_w_uq, mla_w_uk, mla_w_uv, mla_w_out,
           ple_w_proj, ple_w_gate):
    B, S, D = x.shape
    T = B * S
    tm = min(512, T)
    tq = 2 * tm
    assert S % tq == 0
    n_chunks = min(2, S // RET_CHUNK)

    cos, sin, tab, tabt = _rope_tables(positions)
    xf = x.reshape(T, D)
    pf = p.reshape(DEPTH, T, PLE_DIM)

    for i in range(DEPTH):
        j = i // 2
        g_pre = pre_norm_g[i].reshape(1, D)
        if i % 2 == 0:
            q, k, v, sg = _ret_in(xf, g_pre, ret_w_in[j].astype(BF16), cos, sin, tm)
            og = _ret_mix(q.reshape(B, S, -1), k.reshape(B, S, -1), v.reshape(B, S, -1),
                          sg.reshape(B, S, -1), ret_gn_g[j].reshape(1, RET_WIDTH), n_chunks)
            w_out = ret_w_out[j]
        else:
            wuk = mla_w_uk[j].reshape(KV_LORA, MLA_HEADS * MLA_NOPE).astype(BF16)
            wuv_t = mla_w_uv[j].reshape(KV_LORA, MLA_WIDTH).T.astype(BF16)
            cq, k, vt, sg = _mla_in(xf, g_pre, _prep_mla_w_in(mla_w_in[j]),
                                    mla_q_norm_g[j].reshape(1, Q_LORA),
                                    mla_kv_norm_g[j].reshape(1, KV_LORA), tab, wuk, wuv_t, tm)
            qt = _mla_q(cq, mla_w_uq[j].T.astype(BF16), tabt, tq)
            og = _attn(qt, k.reshape(B, S, -1), vt, sg.reshape(B, S, -1), tq, ATTN_HEADS_PER_STEP)
            w_out = mla_w_out[j]
        xf = _out(og.reshape(T, -1), xf, pf, i, w_out.astype(BF16),
                  post_norm_g[i].reshape(1, D), ple_w_proj[i].astype(BF16),
                  ple_w_gate[i].astype(BF16), tm)
    return xf.reshape(B, S, D)
```

```python
import functools
import math

import jax
import jax.numpy as jnp
from jax import lax
from jax.experimental import pallas as pl
from jax.experimental.pallas import tpu as pltpu

D_MODEL = 1024
DEPTH = 4
ROPE_BASE = 10000.0
NORM_EPS = 1e-6

RET_HEADS = 4
RET_QK = 256
RET_V = 512
RET_WIDTH = RET_HEADS * RET_V
RET_CHUNK = 256
RET_LOG_G = tuple(math.log(1.0 - 2.0 ** (-5.0 - h)) for h in range(RET_HEADS))

MLA_HEADS = 16
MLA_NOPE = 128
MLA_ROPE = 64
MLA_V = 128
Q_LORA = 768
KV_LORA = 256
MLA_WIDTH = MLA_HEADS * MLA_V
MLA_QK_PAD = 256
MLA_V_AUG = MLA_V + 16
ATTN_HEADS_PER_STEP = 2
ATTN_QBLOCKS_PER_STEP = 2
PLE_DIM = 256

LANES = 128
VMEM_LIMIT = 56 * 1024 * 1024

BF16 = jnp.bfloat16
F32 = jnp.float32

_NT = (((1,), (1,)), ((), ()))
_TN = (((0,), (0,)), ((), ()))


def _dot(a, b):
    return jnp.dot(a, b, preferred_element_type=F32)


def _rms(x, g):
    return x * lax.rsqrt(jnp.mean(x * x, axis=-1, keepdims=True) + NORM_EPS) * g


def _silu(x):
    return x * jax.nn.sigmoid(x)


def _params(*sem):
    return pltpu.CompilerParams(dimension_semantics=sem, vmem_limit_bytes=VMEM_LIMIT)


def _row_spec(tm, width):
    return pl.BlockSpec((tm, width), lambda i: (i, 0))


def _full_spec(shape):
    return pl.BlockSpec(shape, lambda *_: (0,) * len(shape))


def _rope_table_kernel(pos_ref, posr_ref, fret_ref, fmlac_ref,
                       cos_ref, sin_ref, tab_ref, tabt_ref):
    pos = pos_ref[...].astype(F32)
    ang = pos * fret_ref[...]
    cos_ref[...] = jnp.cos(ang)
    sin_ref[...] = jnp.sin(ang)
    ang2 = fmlac_ref[...] * posr_ref[...].astype(F32)
    c = jnp.cos(ang2)
    s = jnp.sin(ang2)
    tabt = jnp.concatenate([c, c, s, s], axis=0)
    tabt_ref[...] = tabt
    tab_ref[...] = tabt.T


def _rope_tables(positions):
    T = positions.size
    ts = min(1024, T)
    half_r = RET_QK // 2
    half_m = MLA_ROPE // 2
    f_ret = (ROPE_BASE ** (-jnp.arange(half_r, dtype=F32) / half_r)).reshape(1, half_r)
    f_mla = (ROPE_BASE ** (-jnp.arange(half_m, dtype=F32) / half_m)).reshape(half_m, 1)
    out = jax.ShapeDtypeStruct((T, LANES), F32)
    return pl.pallas_call(
        _rope_table_kernel,
        grid=(T // ts,),
        in_specs=[_row_spec(ts, 1), pl.BlockSpec((1, ts), lambda i: (0, i)),
                  _full_spec((1, LANES)), _full_spec((half_m, 1))],
        out_specs=[_row_spec(ts, LANES)] * 3 + [pl.BlockSpec((LANES, ts), lambda i: (0, i))],
        out_shape=[out, out, out, jax.ShapeDtypeStruct((LANES, T), F32)],
        compiler_params=_params("parallel"),
        name="rope_tables",
    )(positions.reshape(T, 1), positions.reshape(1, T), f_ret, f_mla)


def _ret_in_kernel(x_ref, g_ref, w_ref, cos_ref, sin_ref, q_ref, k_ref, v_ref, sg_ref):
    h = _rms(x_ref[...], g_ref[...]).astype(BF16)
    cos = cos_ref[...]
    sin = sin_ref[...]
    half = RET_QK // 2
    k_scale = RET_QK ** -0.5
    for j in range(2 * RET_HEADS):
        z = _dot(h, w_ref[:, j * RET_QK:(j + 1) * RET_QK])
        x1 = z[:, :half]
        x2 = z[:, half:]
        r1 = x1 * cos - x2 * sin
        r2 = x2 * cos + x1 * sin
        if j < RET_HEADS:
            q_ref[:, j * RET_QK:j * RET_QK + half] = r1.astype(BF16)
            q_ref[:, j * RET_QK + half:(j + 1) * RET_QK] = r2.astype(BF16)
        else:
            c0 = (j - RET_HEADS) * RET_QK
            k_ref[:, c0:c0 + half] = (r1 * k_scale).astype(BF16)
            k_ref[:, c0 + half:c0 + RET_QK] = (r2 * k_scale).astype(BF16)
    v0 = 2 * RET_HEADS * RET_QK
    g0 = v0 + RET_WIDTH
    for c in range(RET_HEADS):
        cs = slice(c * RET_V, (c + 1) * RET_V)
        v_ref[:, cs] = _dot(h, w_ref[:, v0 + c * RET_V:v0 + (c + 1) * RET_V]).astype(BF16)
        gate = _dot(h, w_ref[:, g0 + c * RET_V:g0 + (c + 1) * RET_V])
        sg_ref[:, cs] = _silu(gate).astype(BF16)


def _ret_in(x, g, w, cos, sin, tm):
    T = x.shape[0]
    qk_w = RET_HEADS * RET_QK
    return pl.pallas_call(
        _ret_in_kernel,
        grid=(T // tm,),
        in_specs=[_row_spec(tm, D_MODEL), _full_spec((1, D_MODEL)), _full_spec(w.shape),
                  _row_spec(tm, LANES), _row_spec(tm, LANES)],
        out_specs=[_row_spec(tm, qk_w), _row_spec(tm, qk_w),
                   _row_spec(tm, RET_WIDTH), _row_spec(tm, RET_WIDTH)],
        out_shape=[jax.ShapeDtypeStruct((T, qk_w), BF16), jax.ShapeDtypeStruct((T, qk_w), BF16),
                   jax.ShapeDtypeStruct((T, RET_WIDTH), BF16),
                   jax.ShapeDtypeStruct((T, RET_WIDTH), BF16)],
        compiler_params=_params("parallel"),
        name="ret_in",
    )(x, g, w, cos, sin)


def _ret_mix_kernel(q_ref, k_ref, v_ref, sg_ref, gn_ref, o_ref,
                    state_ref, dmask_ref, qdec_ref, kdec_ref, *, n_chunks):
    C = RET_CHUNK

    @pl.when(pl.program_id(1) == 0)
    def _init():
        state_ref[...] = jnp.zeros_like(state_ref)
        row = lax.broadcasted_iota(jnp.int32, (C, C), 0)
        col = lax.broadcasted_iota(jnp.int32, (C, C), 1)
        rel = (row - col).astype(F32)
        iq = lax.broadcasted_iota(jnp.int32, (C, RET_V), 0).astype(F32)
        ik = lax.broadcasted_iota(jnp.int32, (C, RET_QK), 0).astype(F32)
        for h in range(RET_HEADS):
            lg = RET_LOG_G[h]
            dmask_ref[h] = jnp.where(rel >= 0, jnp.exp(jnp.maximum(rel, 0.0) * lg), 0.0)
            qdec_ref[h] = jnp.exp((iq + 1.0) * lg)
            kdec_ref[h] = jnp.exp((C - 1.0 - ik) * lg)

    for c in range(n_chunks):
        rows = slice(c * C, (c + 1) * C)
        for h in range(RET_HEADS):
            qs = slice(h * RET_QK, (h + 1) * RET_QK)
            vs = slice(h * RET_V, (h + 1) * RET_V)
            qh = q_ref[0, rows, qs]
            kh = k_ref[0, rows, qs]
            vh = v_ref[0, rows, vs]
            a = lax.dot_general(qh, kh, _NT, preferred_element_type=F32) * dmask_ref[h]
            st = state_ref[h]
            o = _dot(a.astype(BF16), vh) + qdec_ref[h] * _dot(qh, st.astype(BF16))
            kd = (kh.astype(F32) * kdec_ref[h]).astype(BF16)
            state_ref[h] = st * math.exp(C * RET_LOG_G[h]) + lax.dot_general(
                kd, vh, _TN, preferred_element_type=F32)
            o = o * lax.rsqrt(jnp.mean(o * o, axis=-1, keepdims=True) + NORM_EPS)
            o = o * gn_ref[:, vs] * sg_ref[0, rows, vs].astype(F32)
            o_ref[0, rows, vs] = o.astype(BF16)


def _ret_mix(q, k, v, sg, gn, n_chunks):
    B, S, _ = q.shape
    rows = n_chunks * RET_CHUNK
    qk_w = RET_HEADS * RET_QK

    def spec(width):
        return pl.BlockSpec((1, rows, width), lambda b, j: (b, j, 0))

    return pl.pallas_call(
        functools.partial(_ret_mix_kernel, n_chunks=n_chunks),
        grid=(B, S // rows),
        in_specs=[spec(qk_w), spec(qk_w), spec(RET_WIDTH), spec(RET_WIDTH),
                  _full_spec((1, RET_WIDTH))],
        out_specs=spec(RET_WIDTH),
        out_shape=jax.ShapeDtypeStruct((B, S, RET_WIDTH), BF16),
        scratch_shapes=[pltpu.VMEM((RET_HEADS, RET_QK, RET_V), F32),
                        pltpu.VMEM((RET_HEADS, RET_CHUNK, RET_CHUNK), F32),
                        pltpu.VMEM((RET_HEADS, RET_CHUNK, RET_V), F32),
                        pltpu.VMEM((RET_HEADS, RET_CHUNK, RET_QK), F32)],
        compiler_params=_params("arbitrary", "arbitrary"),
        name="ret_mix",
    )(q, k, v, sg, gn)


def _mla_in_kernel(x_ref, g_ref, w_ref, qg_ref, kvg_ref, tab_ref, wuk_ref, wuv_ref,
                   cq_ref, k_ref, v_ref, sg_ref):
    h = _rms(x_ref[...], g_ref[...]).astype(BF16)
    o1 = Q_LORA
    o2 = o1 + KV_LORA
    o3 = o2 + LANES
    q_scale = (MLA_NOPE + MLA_ROPE) ** -0.5 * math.log2(math.e)
    cq_ref[...] = _rms(_dot(h, w_ref[:, :o1]), qg_ref[...] * q_scale).astype(BF16)
    c_kv = _rms(_dot(h, w_ref[:, o1:o2]), kvg_ref[...]).astype(BF16)
    t = _dot(h, w_ref[:, o2:o3]) * tab_ref[...]
    t = t + pltpu.roll(t, MLA_ROPE, 1)
    lane = lax.broadcasted_iota(jnp.int32, t.shape, 1)
    k_tail = jnp.where(lane < MLA_ROPE, t, 0.0).astype(BF16)
    k_nope = _dot(c_kv, wuk_ref[...]).astype(BF16)
    for hd in range(MLA_HEADS):
        c0 = hd * MLA_QK_PAD
        k_ref[:, c0:c0 + MLA_NOPE] = k_nope[:, hd * MLA_NOPE:(hd + 1) * MLA_NOPE]
        k_ref[:, c0 + MLA_NOPE:c0 + MLA_QK_PAD] = k_tail
    vt = lax.dot_general(wuv_ref[...], c_kv, _NT, preferred_element_type=F32).astype(BF16)
    ones = jnp.ones((MLA_V_AUG - MLA_V, vt.shape[1]), BF16)
    for hd in range(MLA_HEADS):
        v_ref[0, hd, :MLA_V, :] = vt[hd * MLA_V:(hd + 1) * MLA_V]
        v_ref[0, hd, MLA_V:, :] = ones
    chunk = 512
    for c in range(MLA_WIDTH // chunk):
        gate = _dot(h, w_ref[:, o3 + c * chunk:o3 + (c + 1) * chunk])
        sg_ref[:, c * chunk:(c + 1) * chunk] = _silu(gate).astype(BF16)


def _mla_in(x, g, w, qg, kvg, tab, wuk, wuv, tm):
    T = x.shape[0]
    kw = MLA_HEADS * MLA_QK_PAD
    return pl.pallas_call(
        _mla_in_kernel,
        grid=(T // tm,),
        in_specs=[_row_spec(tm, D_MODEL), _full_spec((1, D_MODEL)), _full_spec(w.shape),
                  _full_spec((1, Q_LORA)), _full_spec((1, KV_LORA)), _row_spec(tm, LANES),
                  _full_spec(wuk.shape), _full_spec(wuv.shape)],
        out_specs=[_row_spec(tm, Q_LORA), _row_spec(tm, kw),
                   pl.BlockSpec((1, MLA_HEADS, MLA_V_AUG, tm), lambda i: (i, 0, 0, 0)),
                   _row_spec(tm, MLA_WIDTH)],
        out_shape=[jax.ShapeDtypeStruct((T, Q_LORA), BF16), jax.ShapeDtypeStruct((T, kw), BF16),
                   jax.ShapeDtypeStruct((T // tm, MLA_HEADS, MLA_V_AUG, tm), BF16),
                   jax.ShapeDtypeStruct((T, MLA_WIDTH), BF16)],
        compiler_params=_params("parallel"),
        name="mla_in",
    )(x, g, w, qg, kvg, tab, wuk, wuv)


def _mla_q_kernel(cq_ref, w_ref, tabt_ref, q_ref):
    cq = cq_ref[...]
    half = MLA_ROPE // 2
    cos = tabt_ref[:half, :]
    sin = tabt_ref[MLA_ROPE:MLA_ROPE + half, :]
    group = 4
    per_head = MLA_NOPE + MLA_ROPE
    rows = group * per_head
    for g in range(MLA_HEADS // group):
        z = lax.dot_general(w_ref[g * rows:(g + 1) * rows, :], cq, _NT,
                            preferred_element_type=F32)
        for r in range(group):
            hd = g * group + r
            r0 = r * per_head
            q_ref[0, hd, :MLA_NOPE, :] = z[r0:r0 + MLA_NOPE].astype(BF16)
            x1 = z[r0 + MLA_NOPE:r0 + MLA_NOPE + half]
            x2 = z[r0 + MLA_NOPE + half:r0 + per_head]
            q_ref[0, hd, MLA_NOPE:MLA_NOPE + half, :] = (x1 * cos - x2 * sin).astype(BF16)
            q_ref[0, hd, MLA_NOPE + half:per_head, :] = (x2 * cos + x1 * sin).astype(BF16)
            q_ref[0, hd, per_head:, :] = jnp.zeros((MLA_QK_PAD - per_head, z.shape[1]), BF16)


def _mla_q(cq, wt, tabt, tq):
    T = cq.shape[0]
    return pl.pallas_call(
        _mla_q_kernel,
        grid=(T // tq,),
        in_specs=[_row_spec(tq, Q_LORA), _full_spec(wt.shape),
                  pl.BlockSpec((LANES, tq), lambda i: (0, i))],
        out_specs=pl.BlockSpec((1, MLA_HEADS, MLA_QK_PAD, tq), lambda i: (i, 0, 0, 0)),
        out_shape=jax.ShapeDtypeStruct((T // tq, MLA_HEADS, MLA_QK_PAD, tq), BF16),
        compiler_params=_params("parallel"),
        name="mla_q",
    )(cq, wt, tabt)


def _attn_kernel(qt_ref, k_ref, vt_ref, sg_ref, o_ref, acc_ref, s0_ref, s1_ref,
                 p0_ref, p1_ref, *, tq, hp, qb):
    tk = tq // 2

    def one_block(i, slot):
        rows = slice(slot * tq, (slot + 1) * tq)

        def stage_a(a, key0, s_ref):
            k = k_ref[0, pl.ds(pl.multiple_of(key0, tk), tk), a * MLA_QK_PAD:(a + 1) * MLA_QK_PAD]
            s = _dot(k, qt_ref[slot, a])
            s_ref[a] = s
            return jnp.max(s, axis=0, keepdims=True)

        def stage_b(a, s_ref, p_ref, m, cm):
            m_new = jnp.maximum(m, cm)
            p_ref[a] = jnp.exp2((s_ref[a] - m_new).astype(BF16))
            return m_new, jnp.exp2(m - m_new)

        def stage_c(a, p_ref, vt, alpha):
            acc_ref[slot, a] = alpha * acc_ref[slot, a] + _dot(vt, p_ref[a])

        tri = (lax.broadcasted_iota(jnp.int32, (tk, tk), 0)
               <= lax.broadcasted_iota(jnp.int32, (tk, tk), 1))
        state = []
        for a in range(hp):
            k_hi = k_ref[0, pl.ds(pl.multiple_of(i * tq + tk, tk), tk),
                         a * MLA_QK_PAD:(a + 1) * MLA_QK_PAD]
            s_hi = jnp.where(tri, _dot(k_hi, qt_ref[slot, a, :, tk:]), -1e30)
            cm_hi = jnp.max(s_hi, axis=0, keepdims=True)
            p0_ref[a, :, :tk] = jnp.zeros((tk, tk), BF16)
            p0_ref[a, :, tk:] = jnp.exp2((s_hi - cm_hi).astype(BF16))
            m = jnp.concatenate([jnp.full((1, tk), -jnp.inf, F32), cm_hi], axis=1)
            k_lo = k_ref[0, pl.ds(pl.multiple_of(i * tq, tk), tk),
                         a * MLA_QK_PAD:(a + 1) * MLA_QK_PAD]
            s_lo = _dot(k_lo, qt_ref[slot, a])
            s_tri = jnp.where(tri, s_lo[:, :tk], -1e30)
            s1_ref[a, :, :tk] = s_tri
            s1_ref[a, :, tk:] = s_lo[:, tk:]
            cm1 = jnp.concatenate([jnp.max(s_tri, axis=0, keepdims=True),
                                   jnp.max(s_lo[:, tk:], axis=0, keepdims=True)], axis=1)
            state.append((m, cm1, jnp.zeros((1, tq), F32)))
        acc_ref[slot] = jnp.zeros(acc_ref.shape[1:], F32)

        def pair(jj, state):
            blk = jnp.where(jj == 0, 2 * i + 1, 2 * (jj - 1))
            blk_next = jnp.where(jj == 0, 2 * i, 2 * (jj - 1) + 1)
            heads = range(hp)
            ms = [st[0] for st in state]
            cm1s = [st[1] for st in state]
            alpha0s = [st[2] for st in state]
            alpha1s = [None] * hp
            cm0s = [stage_a(a, jj * tq, s0_ref) for a in heads]
            for a in heads:
                stage_c(a, p0_ref, vt_ref[blk, a], alpha0s[a])
            for a in heads:
                ms[a], alpha1s[a] = stage_b(a, s1_ref, p1_ref, ms[a], cm1s[a])
            cm1s = [stage_a(a, jj * tq + tk, s1_ref) for a in heads]
            for a in heads:
                stage_c(a, p1_ref, vt_ref[blk_next, a], alpha1s[a])
            for a in heads:
                ms[a], alpha0s[a] = stage_b(a, s0_ref, p0_ref, ms[a], cm0s[a])
            return tuple(zip(ms, cm1s, alpha0s))

        state = lax.fori_loop(0, i, pair, tuple(state))
        blk = jnp.where(i == 0, 1, 2 * (i - 1))
        blk_next = jnp.where(i == 0, 0, 2 * (i - 1) + 1)
        for a in range(hp):
            m, cm1, alpha0 = state[a]
            _, alpha1 = stage_b(a, s1_ref, p1_ref, m, cm1)
            stage_c(a, p0_ref, vt_ref[blk, a], alpha0)
            stage_c(a, p1_ref, vt_ref[blk_next, a], alpha1)
            acc = acc_ref[slot, a]
            o = (acc[:MLA_V] / acc[MLA_V:MLA_V + 1]).T
            cs = slice(a * MLA_V, (a + 1) * MLA_V)
            o_ref[0, rows, cs] = (o * sg_ref[0, rows, cs].astype(F32)).astype(BF16)

    for slot in range(qb):
        one_block(pl.program_id(2) * qb + slot, slot)


def _attn(qt, k, vt, sg, tq, hp, qb):
    B, S, _ = k.shape
    ng = S // (qb * tq)
    tk = tq // 2
    return pl.pallas_call(
        functools.partial(_attn_kernel, tq=tq, hp=hp, qb=qb),
        grid=(B, MLA_HEADS // hp, ng),
        in_specs=[pl.BlockSpec((qb, hp, MLA_QK_PAD, tq), lambda b, h, i: (b * ng + i, h, 0, 0)),
                  pl.BlockSpec((1, S, hp * MLA_QK_PAD), lambda b, h, i: (b, 0, h)),
                  pl.BlockSpec((S // tk, hp, MLA_V_AUG, tk), lambda b, h, i: (b, h, 0, 0)),
                  pl.BlockSpec((1, qb * tq, hp * MLA_V), lambda b, h, i: (b, i, h))],
        out_specs=pl.BlockSpec((1, qb * tq, hp * MLA_V), lambda b, h, i: (b, i, h)),
        out_shape=jax.ShapeDtypeStruct((B, S, MLA_WIDTH), BF16),
        scratch_shapes=[pltpu.VMEM((qb, hp, MLA_V_AUG, tq), F32),
                        pltpu.VMEM((hp, tk, tq), F32), pltpu.VMEM((hp, tk, tq), F32),
                        pltpu.VMEM((hp, tk, tq), BF16), pltpu.VMEM((hp, tk, tq), BF16)],
        compiler_params=_params("parallel", "parallel", "arbitrary"),
        name="mla_attn",
    )(qt, k, vt, sg)


def _out_kernel(og_ref, x_ref, p_ref, wo_ref, pg_ref, wp_ref, wg_ref, out_ref):
    y = _dot(og_ref[...], wo_ref[...])
    x1 = x_ref[...] + _rms(y, pg_ref[...])
    pe = _dot(p_ref[...].astype(BF16), wp_ref[...])
    gt = jax.nn.sigmoid(_dot(x1.astype(BF16), wg_ref[...]))
    out_ref[...] = x1 + pe * gt


def _out(og, x, p, layer, wo, pg, wp, wg, tm):
    T = x.shape[0]
    width = og.shape[1]
    return pl.pallas_call(
        _out_kernel,
        grid=(T // tm,),
        in_specs=[_row_spec(tm, width), _row_spec(tm, D_MODEL),
                  pl.BlockSpec((None, tm, PLE_DIM), lambda i: (layer, i, 0)),
                  _full_spec(wo.shape), _full_spec((1, D_MODEL)),
                  _full_spec(wp.shape), _full_spec(wg.shape)],
        out_specs=_row_spec(tm, D_MODEL),
        out_shape=jax.ShapeDtypeStruct((T, D_MODEL), F32),
        compiler_params=_params("parallel"),
        name="branch_out",
    )(og, x, p, wo, pg, wp, wg)


def _rotate_half_cols(w):
    half = w.shape[-1] // 2
    return jnp.concatenate([-w[..., half:], w[..., :half]], axis=-1)


def _prep_mla_w_in(w):
    o2 = Q_LORA + KV_LORA
    o3 = o2 + MLA_ROPE
    kr = w[:, o2:o3]
    return jnp.concatenate([w[:, :o3], _rotate_half_cols(kr), w[:, o3:]], axis=1).astype(BF16)


def kernel(x, p, positions, pre_norm_g, post_norm_g, ret_w_in, ret_gn_g, ret_w_out, mla_w_in,
           mla_q_norm_g, mla_kv_norm_g, mla_w_uq, mla_w_uk, mla_w_uv, mla_w_out,
           ple_w_proj, ple_w_gate):
    B, S, D = x.shape
    T = B * S
    tm = min(512, T)
    tq = 2 * tm
    assert S % tq == 0
    n_chunks = min(2, S // RET_CHUNK)

    cos, sin, tab, tabt = _rope_tables(positions)
    xf = x.reshape(T, D)
    pf = p.reshape(DEPTH, T, PLE_DIM)

    for i in range(DEPTH):
        j = i // 2
        g_pre = pre_norm_g[i].reshape(1, D)
        if i % 2 == 0:
            q, k, v, sg = _ret_in(xf, g_pre, ret_w_in[j].astype(BF16), cos, sin, tm)
            og = _ret_mix(q.reshape(B, S, -1), k.reshape(B, S, -1), v.reshape(B, S, -1),
                          sg.reshape(B, S, -1), ret_gn_g[j].reshape(1, RET_WIDTH), n_chunks)
            w_out = ret_w_out[j]
        else:
            wuk = mla_w_uk[j].reshape(KV_LORA, MLA_HEADS * MLA_NOPE).astype(BF16)
            wuv_t = mla_w_uv[j].reshape(KV_LORA, MLA_WIDTH).T.astype(BF16)
            cq, k, vt, sg = _mla_in(xf, g_pre, _prep_mla_w_in(mla_w_in[j]),
                                    mla_q_norm_g[j].reshape(1, Q_LORA),
                                    mla_kv_norm_g[j].reshape(1, KV_LORA), tab, wuk, wuv_t, tm)
            qt = _mla_q(cq, mla_w_uq[j].T.astype(BF16), tabt, tq)
            og = _attn(qt, k.reshape(B, S, -1), vt, sg.reshape(B, S, -1), tq, ATTN_HEADS_PER_STEP,
                       ATTN_QBLOCKS_PER_STEP)
            w_out = mla_w_out[j]
        xf = _out(og.reshape(T, -1), xf, pf, i, w_out.astype(BF16),
                  post_norm_g[i].reshape(1, D), ple_w_proj[i].astype(BF16),
                  ple_w_gate[i].astype(BF16), tm)
    return xf.reshape(B, S, D)
```

```python
import functools
import math

import jax
import jax.numpy as jnp
from jax import lax
from jax.experimental import pallas as pl
from jax.experimental.pallas import tpu as pltpu

D_MODEL = 1024
DEPTH = 4
ROPE_BASE = 10000.0
NORM_EPS = 1e-6

RET_HEADS = 4
RET_QK = 256
RET_V = 512
RET_WIDTH = RET_HEADS * RET_V
RET_CHUNK = 256
RET_LOG_G = tuple(math.log(1.0 - 2.0 ** (-5.0 - h)) for h in range(RET_HEADS))

MLA_HEADS = 16
MLA_NOPE = 128
MLA_ROPE = 64
MLA_V = 128
Q_LORA = 768
KV_LORA = 256
MLA_WIDTH = MLA_HEADS * MLA_V
MLA_QK_PAD = 256
MLA_V_AUG = MLA_V + 16
ATTN_HEADS_PER_STEP = 2
ATTN_QBLOCKS_PER_STEP = 2
PLE_DIM = 256

LANES = 128
VMEM_LIMIT = 56 * 1024 * 1024

BF16 = jnp.bfloat16
F32 = jnp.float32

_NT = (((1,), (1,)), ((), ()))
_TN = (((0,), (0,)), ((), ()))


def _dot(a, b):
    return jnp.dot(a, b, preferred_element_type=F32)


def _rms(x, g):
    return x * lax.rsqrt(jnp.mean(x * x, axis=-1, keepdims=True) + NORM_EPS) * g


def _silu(x):
    return x * jax.nn.sigmoid(x)


def _params(*sem):
    return pltpu.CompilerParams(dimension_semantics=sem, vmem_limit_bytes=VMEM_LIMIT)


def _row_spec(tm, width):
    return pl.BlockSpec((tm, width), lambda i: (i, 0))


def _full_spec(shape):
    return pl.BlockSpec(shape, lambda *_: (0,) * len(shape), pipeline_mode=pl.Buffered(1))


def _rope_table_kernel(pos_ref, posr_ref, fret_ref, fmlac_ref,
                       cos_ref, sin_ref, tab_ref, tabt_ref):
    pos = pos_ref[...].astype(F32)
    ang = pos * fret_ref[...]
    cos_ref[...] = jnp.cos(ang)
    sin_ref[...] = jnp.sin(ang)
    ang2 = fmlac_ref[...] * posr_ref[...].astype(F32)
    c = jnp.cos(ang2)
    s = jnp.sin(ang2)
    tabt = jnp.concatenate([c, c, s, s], axis=0)
    tabt_ref[...] = tabt
    tab_ref[...] = tabt.T


def _rope_tables(positions):
    T = positions.size
    ts = min(1024, T)
    half_r = RET_QK // 2
    half_m = MLA_ROPE // 2
    f_ret = (ROPE_BASE ** (-jnp.arange(half_r, dtype=F32) / half_r)).reshape(1, half_r)
    f_mla = (ROPE_BASE ** (-jnp.arange(half_m, dtype=F32) / half_m)).reshape(half_m, 1)
    out = jax.ShapeDtypeStruct((T, LANES), F32)
    return pl.pallas_call(
        _rope_table_kernel,
        grid=(T // ts,),
        in_specs=[_row_spec(ts, 1), pl.BlockSpec((1, ts), lambda i: (0, i)),
                  _full_spec((1, LANES)), _full_spec((half_m, 1))],
        out_specs=[_row_spec(ts, LANES)] * 3 + [pl.BlockSpec((LANES, ts), lambda i: (0, i))],
        out_shape=[out, out, out, jax.ShapeDtypeStruct((LANES, T), F32)],
        compiler_params=_params("parallel"),
        name="rope_tables",
    )(positions.reshape(T, 1), positions.reshape(1, T), f_ret, f_mla)


def _ret_in_kernel(x_ref, g_ref, w_ref, cos_ref, sin_ref, q_ref, k_ref, v_ref, sg_ref):
    x = x_ref[...]
    h = (x * g_ref[...]).astype(BF16)
    r = lax.rsqrt(jnp.mean(x * x, axis=-1, keepdims=True) + NORM_EPS)
    half = RET_QK // 2
    cos_q = cos_ref[...] * r
    sin_q = sin_ref[...] * r
    cos_k = cos_q * RET_QK ** -0.5
    sin_k = sin_q * RET_QK ** -0.5
    for j in range(2 * RET_HEADS):
        z = _dot(h, w_ref[:, j * RET_QK:(j + 1) * RET_QK])
        x1 = z[:, :half]
        x2 = z[:, half:]
        if j < RET_HEADS:
            q_ref[:, j * RET_QK:j * RET_QK + half] = (x1 * cos_q - x2 * sin_q).astype(BF16)
            q_ref[:, j * RET_QK + half:(j + 1) * RET_QK] = (x2 * cos_q + x1 * sin_q).astype(BF16)
        else:
            c0 = (j - RET_HEADS) * RET_QK
            k_ref[:, c0:c0 + half] = (x1 * cos_k - x2 * sin_k).astype(BF16)
            k_ref[:, c0 + half:c0 + RET_QK] = (x2 * cos_k + x1 * sin_k).astype(BF16)
    v0 = 2 * RET_HEADS * RET_QK
    g0 = v0 + RET_WIDTH
    for c in range(RET_HEADS):
        cs = slice(c * RET_V, (c + 1) * RET_V)
        v_ref[:, cs] = (_dot(h, w_ref[:, v0 + c * RET_V:v0 + (c + 1) * RET_V]) * r).astype(BF16)
        gate = _dot(h, w_ref[:, g0 + c * RET_V:g0 + (c + 1) * RET_V]) * r
        sg_ref[:, cs] = _silu(gate).astype(BF16)


def _ret_in(x, g, w, cos, sin, tm):
    T = x.shape[0]
    qk_w = RET_HEADS * RET_QK
    return pl.pallas_call(
        _ret_in_kernel,
        grid=(T // tm,),
        in_specs=[_row_spec(tm, D_MODEL), _full_spec((1, D_MODEL)), _full_spec(w.shape),
                  _row_spec(tm, LANES), _row_spec(tm, LANES)],
        out_specs=[_row_spec(tm, qk_w), _row_spec(tm, qk_w),
                   _row_spec(tm, RET_WIDTH), _row_spec(tm, RET_WIDTH)],
        out_shape=[jax.ShapeDtypeStruct((T, qk_w), BF16), jax.ShapeDtypeStruct((T, qk_w), BF16),
                   jax.ShapeDtypeStruct((T, RET_WIDTH), BF16),
                   jax.ShapeDtypeStruct((T, RET_WIDTH), BF16)],
        compiler_params=_params("parallel"),
        name="ret_in",
    )(x, g, w, cos, sin)


def _ret_mix_kernel(q_ref, k_ref, v_ref, sg_ref, gn_ref, o_ref,
                    state_ref, dmask_ref, qdec_ref, kdec_ref, *, n_chunks):
    C = RET_CHUNK

    @pl.when(pl.program_id(1) == 0)
    def _init():
        state_ref[...] = jnp.zeros_like(state_ref)
        row = lax.broadcasted_iota(jnp.int32, (C, C), 0)
        col = lax.broadcasted_iota(jnp.int32, (C, C), 1)
        rel = (row - col).astype(F32)
        iq = lax.broadcasted_iota(jnp.int32, (C, RET_V), 0).astype(F32)
        ik = lax.broadcasted_iota(jnp.int32, (C, RET_QK), 0).astype(F32)
        for h in range(RET_HEADS):
            lg = RET_LOG_G[h]
            dmask_ref[h] = jnp.where(rel >= 0, jnp.exp(jnp.maximum(rel, 0.0) * lg), 0.0)
            qdec_ref[h] = jnp.exp((iq + 1.0) * lg)
            kdec_ref[h] = jnp.exp((C - 1.0 - ik) * lg)

    for c in range(n_chunks):
        rows = slice(c * C, (c + 1) * C)
        for h in range(RET_HEADS):
            qs = slice(h * RET_QK, (h + 1) * RET_QK)
            vs = slice(h * RET_V, (h + 1) * RET_V)
            qh = q_ref[0, rows, qs]
            kh = k_ref[0, rows, qs]
            vh = v_ref[0, rows, vs]
            a = lax.dot_general(qh, kh, _NT, preferred_element_type=F32) * dmask_ref[h]
            st = state_ref[h]
            o = _dot(a.astype(BF16), vh) + qdec_ref[h] * _dot(qh, st.astype(BF16))
            kd = (kh.astype(F32) * kdec_ref[h]).astype(BF16)
            state_ref[h] = st * math.exp(C * RET_LOG_G[h]) + lax.dot_general(
                kd, vh, _TN, preferred_element_type=F32)
            o = o * lax.rsqrt(jnp.mean(o * o, axis=-1, keepdims=True) + NORM_EPS)
            o = o * gn_ref[:, vs] * sg_ref[0, rows, vs].astype(F32)
            o_ref[0, rows, vs] = o.astype(BF16)


def _ret_mix(q, k, v, sg, gn, n_chunks):
    B, S, _ = q.shape
    rows = n_chunks * RET_CHUNK
    qk_w = RET_HEADS * RET_QK

    def spec(width):
        return pl.BlockSpec((1, rows, width), lambda b, j: (b, j, 0))

    return pl.pallas_call(
        functools.partial(_ret_mix_kernel, n_chunks=n_chunks),
        grid=(B, S // rows),
        in_specs=[spec(qk_w), spec(qk_w), spec(RET_WIDTH), spec(RET_WIDTH),
                  _full_spec((1, RET_WIDTH))],
        out_specs=spec(RET_WIDTH),
        out_shape=jax.ShapeDtypeStruct((B, S, RET_WIDTH), BF16),
        scratch_shapes=[pltpu.VMEM((RET_HEADS, RET_QK, RET_V), F32),
                        pltpu.VMEM((RET_HEADS, RET_CHUNK, RET_CHUNK), F32),
                        pltpu.VMEM((RET_HEADS, RET_CHUNK, RET_V), F32),
                        pltpu.VMEM((RET_HEADS, RET_CHUNK, RET_QK), F32)],
        compiler_params=_params("arbitrary", "arbitrary"),
        name="ret_mix",
    )(q, k, v, sg, gn)


def _mla_in_kernel(x_ref, g_ref, w_ref, qg_ref, kvg_ref, tab_ref, wuk_ref, wuv_ref,
                   cq_ref, k_ref, v_ref, sg_ref):
    x = x_ref[...]
    h = (x * g_ref[...]).astype(BF16)
    r = lax.rsqrt(jnp.mean(x * x, axis=-1, keepdims=True) + NORM_EPS)
    o1 = Q_LORA
    o2 = o1 + KV_LORA
    o3 = o2 + LANES
    q_scale = (MLA_NOPE + MLA_ROPE) ** -0.5 * math.log2(math.e)
    cq_ref[...] = _rms(_dot(h, w_ref[:, :o1]) * r, qg_ref[...] * q_scale).astype(BF16)
    c_kv = _rms(_dot(h, w_ref[:, o1:o2]) * r, kvg_ref[...]).astype(BF16)
    t = _dot(h, w_ref[:, o2:o3]) * (tab_ref[...] * r)
    t = t + pltpu.roll(t, MLA_ROPE, 1)
    lane = lax.broadcasted_iota(jnp.int32, t.shape, 1)
    k_tail = jnp.where(lane < MLA_ROPE, t, 0.0).astype(BF16)
    k_nope = _dot(c_kv, wuk_ref[...]).astype(BF16)
    for hd in range(MLA_HEADS):
        c0 = hd * MLA_QK_PAD
        k_ref[:, c0:c0 + MLA_NOPE] = k_nope[:, hd * MLA_NOPE:(hd + 1) * MLA_NOPE]
        k_ref[:, c0 + MLA_NOPE:c0 + MLA_QK_PAD] = k_tail
    vt = lax.dot_general(wuv_ref[...], c_kv, _NT, preferred_element_type=F32).astype(BF16)
    ones = jnp.ones((MLA_V_AUG - MLA_V, vt.shape[1]), BF16)
    for hd in range(MLA_HEADS):
        v_ref[0, hd, :MLA_V, :] = vt[hd * MLA_V:(hd + 1) * MLA_V]
        v_ref[0, hd, MLA_V:, :] = ones
    chunk = 512
    for c in range(MLA_WIDTH // chunk):
        gate = _dot(h, w_ref[:, o3 + c * chunk:o3 + (c + 1) * chunk]) * r
        sg_ref[:, c * chunk:(c + 1) * chunk] = _silu(gate).astype(BF16)


def _mla_in(x, g, w, qg, kvg, tab, wuk, wuv, tm):
    T = x.shape[0]
    kw = MLA_HEADS * MLA_QK_PAD
    return pl.pallas_call(
        _mla_in_kernel,
        grid=(T // tm,),
        in_specs=[_row_spec(tm, D_MODEL), _full_spec((1, D_MODEL)), _full_spec(w.shape),
                  _full_spec((1, Q_LORA)), _full_spec((1, KV_LORA)), _row_spec(tm, LANES),
                  _full_spec(wuk.shape), _full_spec(wuv.shape)],
        out_specs=[_row_spec(tm, Q_LORA), _row_spec(tm, kw),
                   pl.BlockSpec((1, MLA_HEADS, MLA_V_AUG, tm), lambda i: (i, 0, 0, 0)),
                   _row_spec(tm, MLA_WIDTH)],
        out_shape=[jax.ShapeDtypeStruct((T, Q_LORA), BF16), jax.ShapeDtypeStruct((T, kw), BF16),
                   jax.ShapeDtypeStruct((T // tm, MLA_HEADS, MLA_V_AUG, tm), BF16),
                   jax.ShapeDtypeStruct((T, MLA_WIDTH), BF16)],
        compiler_params=_params("parallel"),
        name="mla_in",
    )(x, g, w, qg, kvg, tab, wuk, wuv)


def _mla_q_kernel(cq_ref, w_ref, tabt_ref, q_ref):
    cq = cq_ref[...]
    half = MLA_ROPE // 2
    cos = tabt_ref[:half, :]
    sin = tabt_ref[MLA_ROPE:MLA_ROPE + half, :]
    group = 4
    per_head = MLA_NOPE + MLA_ROPE
    rows = group * per_head
    for g in range(MLA_HEADS // group):
        z = lax.dot_general(w_ref[g * rows:(g + 1) * rows, :], cq, _NT,
                            preferred_element_type=F32)
        for r in range(group):
            hd = g * group + r
            r0 = r * per_head
            q_ref[0, hd, :MLA_NOPE, :] = z[r0:r0 + MLA_NOPE].astype(BF16)
            x1 = z[r0 + MLA_NOPE:r0 + MLA_NOPE + half]
            x2 = z[r0 + MLA_NOPE + half:r0 + per_head]
            q_ref[0, hd, MLA_NOPE:MLA_NOPE + half, :] = (x1 * cos - x2 * sin).astype(BF16)
            q_ref[0, hd, MLA_NOPE + half:per_head, :] = (x2 * cos + x1 * sin).astype(BF16)
            q_ref[0, hd, per_head:, :] = jnp.zeros((MLA_QK_PAD - per_head, z.shape[1]), BF16)


def _mla_q(cq, wt, tabt, tq):
    T = cq.shape[0]
    return pl.pallas_call(
        _mla_q_kernel,
        grid=(T // tq,),
        in_specs=[_row_spec(tq, Q_LORA), _full_spec(wt.shape),
                  pl.BlockSpec((LANES, tq), lambda i: (0, i))],
        out_specs=pl.BlockSpec((1, MLA_HEADS, MLA_QK_PAD, tq), lambda i: (i, 0, 0, 0)),
        out_shape=jax.ShapeDtypeStruct((T // tq, MLA_HEADS, MLA_QK_PAD, tq), BF16),
        compiler_params=_params("parallel"),
        name="mla_q",
    )(cq, wt, tabt)


def _attn_kernel(qt_ref, k_ref, vt_ref, sg_ref, o_ref, acc_ref, s0_ref, s1_ref,
                 p0_ref, p1_ref, *, tq, hp, qb):
    tk = tq // 2

    def one_block(i, slot):
        rows = slice(slot * tq, (slot + 1) * tq)

        def stage_a(a, key0, s_ref):
            k = k_ref[0, pl.ds(pl.multiple_of(key0, tk), tk), a * MLA_QK_PAD:(a + 1) * MLA_QK_PAD]
            s = _dot(k, qt_ref[slot, a])
            s_ref[a] = s
            return jnp.max(s, axis=0, keepdims=True)

        def stage_b(a, s_ref, p_ref, m, cm):
            m_new = jnp.maximum(m, cm)
            p_ref[a] = jnp.exp2((s_ref[a] - m_new).astype(BF16))
            return m_new, jnp.exp2(m - m_new)

        def stage_c(a, p_ref, vt, alpha):
            acc_ref[slot, a] = alpha * acc_ref[slot, a] + _dot(vt, p_ref[a])

        tri = (lax.broadcasted_iota(jnp.int32, (tk, tk), 0)
               <= lax.broadcasted_iota(jnp.int32, (tk, tk), 1))
        state = []
        for a in range(hp):
            k_hi = k_ref[0, pl.ds(pl.multiple_of(i * tq + tk, tk), tk),
                         a * MLA_QK_PAD:(a + 1) * MLA_QK_PAD]
            s_hi = jnp.where(tri, _dot(k_hi, qt_ref[slot, a, :, tk:]), -1e30)
            cm_hi = jnp.max(s_hi, axis=0, keepdims=True)
            p0_ref[a, :, :tk] = jnp.zeros((tk, tk), BF16)
            p0_ref[a, :, tk:] = jnp.exp2((s_hi - cm_hi).astype(BF16))
            m = jnp.concatenate([jnp.full((1, tk), -jnp.inf, F32), cm_hi], axis=1)
            k_lo = k_ref[0, pl.ds(pl.multiple_of(i * tq, tk), tk),
                         a * MLA_QK_PAD:(a + 1) * MLA_QK_PAD]
            s_lo = _dot(k_lo, qt_ref[slot, a])
            s_tri = jnp.where(tri, s_lo[:, :tk], -1e30)
            s1_ref[a, :, :tk] = s_tri
            s1_ref[a, :, tk:] = s_lo[:, tk:]
            cm1 = jnp.concatenate([jnp.max(s_tri, axis=0, keepdims=True),
                                   jnp.max(s_lo[:, tk:], axis=0, keepdims=True)], axis=1)
            state.append((m, cm1, jnp.zeros((1, tq), F32)))
        acc_ref[slot] = jnp.zeros(acc_ref.shape[1:], F32)

        def pair(jj, state):
            blk = jnp.where(jj == 0, 2 * i + 1, 2 * (jj - 1))
            blk_next = jnp.where(jj == 0, 2 * i, 2 * (jj - 1) + 1)
            heads = range(hp)
            ms = [st[0] for st in state]
            cm1s = [st[1] for st in state]
            alpha0s = [st[2] for st in state]
            alpha1s = [None] * hp
            cm0s = [stage_a(a, jj * tq, s0_ref) for a in heads]
            for a in heads:
                stage_c(a, p0_ref, vt_ref[blk, a], alpha0s[a])
            for a in heads:
                ms[a], alpha1s[a] = stage_b(a, s1_ref, p1_ref, ms[a], cm1s[a])
            cm1s = [stage_a(a, jj * tq + tk, s1_ref) for a in heads]
            for a in heads:
                stage_c(a, p1_ref, vt_ref[blk_next, a], alpha1s[a])
            for a in heads:
                ms[a], alpha0s[a] = stage_b(a, s0_ref, p0_ref, ms[a], cm0s[a])
            return tuple(zip(ms, cm1s, alpha0s))

        state = lax.fori_loop(0, i, pair, tuple(state))
        blk = jnp.where(i == 0, 1, 2 * (i - 1))
        blk_next = jnp.where(i == 0, 0, 2 * (i - 1) + 1)
        for a in range(hp):
            m, cm1, alpha0 = state[a]
            _, alpha1 = stage_b(a, s1_ref, p1_ref, m, cm1)
            stage_c(a, p0_ref, vt_ref[blk, a], alpha0)
            stage_c(a, p1_ref, vt_ref[blk_next, a], alpha1)
            acc = acc_ref[slot, a]
            o = (acc[:MLA_V] / acc[MLA_V:MLA_V + 1]).T
            cs = slice(a * MLA_V, (a + 1) * MLA_V)
            o_ref[0, rows, cs] = (o * sg_ref[0, rows, cs].astype(F32)).astype(BF16)

    for slot in range(qb):
        one_block(pl.program_id(2) * qb + slot, slot)


def _attn(qt, k, vt, sg, tq, hp, qb):
    B, S, _ = k.shape
    ng = S // (qb * tq)
    tk = tq // 2
    return pl.pallas_call(
        functools.partial(_attn_kernel, tq=tq, hp=hp, qb=qb),
        grid=(B, MLA_HEADS // hp, ng),
        in_specs=[pl.BlockSpec((qb, hp, MLA_QK_PAD, tq), lambda b, h, i: (b * ng + i, h, 0, 0)),
                  pl.BlockSpec((1, S, hp * MLA_QK_PAD), lambda b, h, i: (b, 0, h)),
                  pl.BlockSpec((S // tk, hp, MLA_V_AUG, tk), lambda b, h, i: (b, h, 0, 0)),
                  pl.BlockSpec((1, qb * tq, hp * MLA_V), lambda b, h, i: (b, i, h))],
        out_specs=pl.BlockSpec((1, qb * tq, hp * MLA_V), lambda b, h, i: (b, i, h)),
        out_shape=jax.ShapeDtypeStruct((B, S, MLA_WIDTH), BF16),
        scratch_shapes=[pltpu.VMEM((qb, hp, MLA_V_AUG, tq), F32),
                        pltpu.VMEM((hp, tk, tq), F32), pltpu.VMEM((hp, tk, tq), F32),
                        pltpu.VMEM((hp, tk, tq), BF16), pltpu.VMEM((hp, tk, tq), BF16)],
        compiler_params=_params("parallel", "parallel", "arbitrary"),
        name="mla_attn",
    )(qt, k, vt, sg)


def _out_kernel(og_ref, x_ref, p_ref, wo_ref, pg_ref, wp_ref, wg_ref, out_ref):
    y = _dot(og_ref[...], wo_ref[...])
    x1 = x_ref[...] + _rms(y, pg_ref[...])
    pe = _dot(p_ref[...].astype(BF16), wp_ref[...])
    gt = jax.nn.sigmoid(_dot(x1.astype(BF16), wg_ref[...]))
    out_ref[...] = x1 + pe * gt


def _out(og, x, p, layer, wo, pg, wp, wg, tm):
    T = x.shape[0]
    width = og.shape[1]
    return pl.pallas_call(
        _out_kernel,
        grid=(T // tm,),
        in_specs=[_row_spec(tm, width), _row_spec(tm, D_MODEL),
                  pl.BlockSpec((None, tm, PLE_DIM), lambda i: (layer, i, 0)),
                  _full_spec(wo.shape), _full_spec((1, D_MODEL)),
                  _full_spec(wp.shape), _full_spec(wg.shape)],
        out_specs=_row_spec(tm, D_MODEL),
        out_shape=jax.ShapeDtypeStruct((T, D_MODEL), F32),
        compiler_params=_params("parallel"),
        name="branch_out",
    )(og, x, p, wo, pg, wp, wg)


def _rotate_half_cols(w):
    half = w.shape[-1] // 2
    return jnp.concatenate([-w[..., half:], w[..., :half]], axis=-1)


def _prep_mla_w_in(w):
    o2 = Q_LORA + KV_LORA
    o3 = o2 + MLA_ROPE
    kr = w[:, o2:o3]
    return jnp.concatenate([w[:, :o3], _rotate_half_cols(kr), w[:, o3:]], axis=1).astype(BF16)


def kernel(x, p, positions, pre_norm_g, post_norm_g, ret_w_in, ret_gn_g, ret_w_out, mla_w_in,
           mla_q_norm_g, mla_kv_norm_g, mla_w_uq, mla_w_uk, mla_w_uv, mla_w_out,
           ple_w_proj, ple_w_gate):
    B, S, D = x.shape
    T = B * S
    tm = min(512, T)
    tm_wide = 2 * tm
    tq = 2 * tm
    assert S % tq == 0
    n_chunks = min(2, S // RET_CHUNK)

    cos, sin, tab, tabt = _rope_tables(positions)
    xf = x.reshape(T, D)
    pf = p.reshape(DEPTH, T, PLE_DIM)

    for i in range(DEPTH):
        j = i // 2
        g_pre = pre_norm_g[i].reshape(1, D)
        if i % 2 == 0:
            q, k, v, sg = _ret_in(xf, g_pre, ret_w_in[j].astype(BF16), cos, sin, tm_wide)
            og = _ret_mix(q.reshape(B, S, -1), k.reshape(B, S, -1), v.reshape(B, S, -1),
                          sg.reshape(B, S, -1), ret_gn_g[j].reshape(1, RET_WIDTH), n_chunks)
            w_out = ret_w_out[j]
        else:
            wuk = mla_w_uk[j].reshape(KV_LORA, MLA_HEADS * MLA_NOPE).astype(BF16)
            wuv_t = mla_w_uv[j].reshape(KV_LORA, MLA_WIDTH).T.astype(BF16)
            cq, k, vt, sg = _mla_in(xf, g_pre, _prep_mla_w_in(mla_w_in[j]),
                                    mla_q_norm_g[j].reshape(1, Q_LORA),
                                    mla_kv_norm_g[j].reshape(1, KV_LORA), tab, wuk, wuv_t, tm)
            qt = _mla_q(cq, mla_w_uq[j].T.astype(BF16), tabt, tq)
            og = _attn(qt, k.reshape(B, S, -1), vt, sg.reshape(B, S, -1), tq, ATTN_HEADS_PER_STEP,
                       ATTN_QBLOCKS_PER_STEP)
            w_out = mla_w_out[j]
        xf = _out(og.reshape(T, -1), xf, pf, i, w_out.astype(BF16),
                  post_norm_g[i].reshape(1, D), ple_w_proj[i].astype(BF16),
                  ple_w_gate[i].astype(BF16), tm_wide)
    return xf.reshape(B, S, D)
```

```python
import functools
import math

import jax
import jax.numpy as jnp
from jax import lax
from jax.experimental import pallas as pl
from jax.experimental.pallas import tpu as pltpu

D_MODEL = 1024
DEPTH = 4
ROPE_BASE = 10000.0
NORM_EPS = 1e-6

RET_HEADS = 4
RET_QK = 256
RET_V = 512
RET_WIDTH = RET_HEADS * RET_V
RET_CHUNK = 256
RET_LOG_G = tuple(math.log(1.0 - 2.0 ** (-5.0 - h)) for h in range(RET_HEADS))

MLA_HEADS = 16
MLA_NOPE = 128
MLA_ROPE = 64
MLA_V = 128
Q_LORA = 768
KV_LORA = 256
MLA_WIDTH = MLA_HEADS * MLA_V
MLA_QK_PAD = 256
MLA_V_AUG = MLA_V + 16
ATTN_HEADS_PER_STEP = 2
ATTN_QBLOCKS_PER_STEP = 2
PLE_DIM = 256

LANES = 128
VMEM_LIMIT = 56 * 1024 * 1024

BF16 = jnp.bfloat16
F32 = jnp.float32

_NT = (((1,), (1,)), ((), ()))
_TN = (((0,), (0,)), ((), ()))


def _dot(a, b):
    return jnp.dot(a, b, preferred_element_type=F32)


def _rms(x, g):
    return x * lax.rsqrt(jnp.mean(x * x, axis=-1, keepdims=True) + NORM_EPS) * g


def _silu(x):
    return x * jax.nn.sigmoid(x)


def _params(*sem):
    return pltpu.CompilerParams(dimension_semantics=sem, vmem_limit_bytes=VMEM_LIMIT)


def _row_spec(tm, width):
    return pl.BlockSpec((tm, width), lambda i: (i, 0))


def _full_spec(shape):
    return pl.BlockSpec(shape, lambda *_: (0,) * len(shape), pipeline_mode=pl.Buffered(1))


def _rope_table_kernel(pos_ref, posr_ref, fret_ref, fmlac_ref,
                       cos_ref, sin_ref, tab_ref, tabt_ref):
    pos = pos_ref[...].astype(F32)
    ang = pos * fret_ref[...]
    cos_ref[...] = jnp.cos(ang)
    sin_ref[...] = jnp.sin(ang)
    ang2 = fmlac_ref[...] * posr_ref[...].astype(F32)
    c = jnp.cos(ang2)
    s = jnp.sin(ang2)
    tabt = jnp.concatenate([c, c, s, s], axis=0)
    tabt_ref[...] = tabt
    tab_ref[...] = tabt.T


def _rope_tables(positions):
    T = positions.size
    ts = min(1024, T)
    half_r = RET_QK // 2
    half_m = MLA_ROPE // 2
    f_ret = (ROPE_BASE ** (-jnp.arange(half_r, dtype=F32) / half_r)).reshape(1, half_r)
    f_mla = (ROPE_BASE ** (-jnp.arange(half_m, dtype=F32) / half_m)).reshape(half_m, 1)
    out = jax.ShapeDtypeStruct((T, LANES), F32)
    return pl.pallas_call(
        _rope_table_kernel,
        grid=(T // ts,),
        in_specs=[_row_spec(ts, 1), pl.BlockSpec((1, ts), lambda i: (0, i)),
                  _full_spec((1, LANES)), _full_spec((half_m, 1))],
        out_specs=[_row_spec(ts, LANES)] * 3 + [pl.BlockSpec((LANES, ts), lambda i: (0, i))],
        out_shape=[out, out, out, jax.ShapeDtypeStruct((LANES, T), F32)],
        compiler_params=_params("parallel"),
        name="rope_tables",
    )(positions.reshape(T, 1), positions.reshape(1, T), f_ret, f_mla)


def _ret_in_kernel(x_ref, g_ref, w_ref, cos_ref, sin_ref, q_ref, k_ref, v_ref, sg_ref):
    x = x_ref[...]
    h = (x * g_ref[...]).astype(BF16)
    r = lax.rsqrt(jnp.mean(x * x, axis=-1, keepdims=True) + NORM_EPS)
    half = RET_QK // 2
    cos_q = cos_ref[...] * r
    sin_q = sin_ref[...] * r
    cos_k = cos_q * RET_QK ** -0.5
    sin_k = sin_q * RET_QK ** -0.5
    for j in range(2 * RET_HEADS):
        z = _dot(h, w_ref[:, j * RET_QK:(j + 1) * RET_QK])
        x1 = z[:, :half]
        x2 = z[:, half:]
        if j < RET_HEADS:
            q_ref[:, j * RET_QK:j * RET_QK + half] = (x1 * cos_q - x2 * sin_q).astype(BF16)
            q_ref[:, j * RET_QK + half:(j + 1) * RET_QK] = (x2 * cos_q + x1 * sin_q).astype(BF16)
        else:
            c0 = (j - RET_HEADS) * RET_QK
            k_ref[:, c0:c0 + half] = (x1 * cos_k - x2 * sin_k).astype(BF16)
            k_ref[:, c0 + half:c0 + RET_QK] = (x2 * cos_k + x1 * sin_k).astype(BF16)
    v0 = 2 * RET_HEADS * RET_QK
    g0 = v0 + RET_WIDTH
    for c in range(RET_HEADS):
        cs = slice(c * RET_V, (c + 1) * RET_V)
        v_ref[:, cs] = (_dot(h, w_ref[:, v0 + c * RET_V:v0 + (c + 1) * RET_V]) * r).astype(BF16)
        gate = _dot(h, w_ref[:, g0 + c * RET_V:g0 + (c + 1) * RET_V]) * r
        sg_ref[:, cs] = _silu(gate).astype(BF16)


def _ret_in(x, g, w, cos, sin, tm):
    T = x.shape[0]
    qk_w = RET_HEADS * RET_QK
    return pl.pallas_call(
        _ret_in_kernel,
        grid=(T // tm,),
        in_specs=[_row_spec(tm, D_MODEL), _full_spec((1, D_MODEL)), _full_spec(w.shape),
                  _row_spec(tm, LANES), _row_spec(tm, LANES)],
        out_specs=[_row_spec(tm, qk_w), _row_spec(tm, qk_w),
                   _row_spec(tm, RET_WIDTH), _row_spec(tm, RET_WIDTH)],
        out_shape=[jax.ShapeDtypeStruct((T, qk_w), BF16), jax.ShapeDtypeStruct((T, qk_w), BF16),
                   jax.ShapeDtypeStruct((T, RET_WIDTH), BF16),
                   jax.ShapeDtypeStruct((T, RET_WIDTH), BF16)],
        compiler_params=_params("parallel"),
        name="ret_in",
    )(x, g, w, cos, sin)


def _ret_mix_kernel(q_ref, k_ref, v_ref, sg_ref, gn_ref, o_ref,
                    state_ref, dmask_ref, qdec_ref, kdec_ref, *, n_chunks):
    C = RET_CHUNK

    @pl.when(pl.program_id(1) == 0)
    def _init():
        state_ref[...] = jnp.zeros_like(state_ref)
        row = lax.broadcasted_iota(jnp.int32, (C, C), 0)
        col = lax.broadcasted_iota(jnp.int32, (C, C), 1)
        rel = (row - col).astype(F32)
        iq = lax.broadcasted_iota(jnp.int32, (C, RET_V), 0).astype(F32)
        ik = lax.broadcasted_iota(jnp.int32, (C, RET_QK), 0).astype(F32)
        for h in range(RET_HEADS):
            lg = RET_LOG_G[h]
            dmask_ref[h] = jnp.where(rel >= 0, jnp.exp(jnp.maximum(rel, 0.0) * lg), 0.0)
            qdec_ref[h] = jnp.exp((iq + 1.0) * lg)
            kdec_ref[h] = jnp.exp((C - 1.0 - ik) * lg)

    for c in range(n_chunks):
        rows = slice(c * C, (c + 1) * C)
        for h in range(RET_HEADS):
            qs = slice(h * RET_QK, (h + 1) * RET_QK)
            vs = slice(h * RET_V, (h + 1) * RET_V)
            qh = q_ref[0, rows, qs]
            kh = k_ref[0, rows, qs]
            a = (lax.dot_general(qh, kh, _NT, preferred_element_type=F32)
                 * dmask_ref[h]).astype(BF16)
            kd = (kh.astype(F32) * kdec_ref[h]).astype(BF16)
            hw = RET_V // 2
            halves, ssq = [], 0.0
            for part in range(2):
                ps = slice(part * hw, (part + 1) * hw)
                cols = slice(h * RET_V + part * hw, h * RET_V + (part + 1) * hw)
                vp = v_ref[0, rows, cols]
                st = state_ref[h, :, ps]
                o = _dot(a, vp) + qdec_ref[h, :, ps] * _dot(qh, st.astype(BF16))
                state_ref[h, :, ps] = st * math.exp(C * RET_LOG_G[h]) + lax.dot_general(
                    kd, vp, _TN, preferred_element_type=F32)
                ssq = ssq + jnp.sum(o * o, axis=-1, keepdims=True)
                halves.append((cols, o))
            scale = lax.rsqrt(ssq * (1.0 / RET_V) + NORM_EPS)
            for cols, o in halves:
                o = o * scale * gn_ref[:, cols] * sg_ref[0, rows, cols].astype(F32)
                o_ref[0, rows, cols] = o.astype(BF16)


def _ret_mix(q, k, v, sg, gn, n_chunks):
    B, S, _ = q.shape
    rows = n_chunks * RET_CHUNK
    qk_w = RET_HEADS * RET_QK

    def spec(width):
        return pl.BlockSpec((1, rows, width), lambda b, j: (b, j, 0))

    return pl.pallas_call(
        functools.partial(_ret_mix_kernel, n_chunks=n_chunks),
        grid=(B, S // rows),
        in_specs=[spec(qk_w), spec(qk_w), spec(RET_WIDTH), spec(RET_WIDTH),
                  _full_spec((1, RET_WIDTH))],
        out_specs=spec(RET_WIDTH),
        out_shape=jax.ShapeDtypeStruct((B, S, RET_WIDTH), BF16),
        scratch_shapes=[pltpu.VMEM((RET_HEADS, RET_QK, RET_V), F32),
                        pltpu.VMEM((RET_HEADS, RET_CHUNK, RET_CHUNK), F32),
                        pltpu.VMEM((RET_HEADS, RET_CHUNK, RET_V), F32),
                        pltpu.VMEM((RET_HEADS, RET_CHUNK, RET_QK), F32)],
        compiler_params=_params("arbitrary", "arbitrary"),
        name="ret_mix",
    )(q, k, v, sg, gn)


def _mla_in_kernel(x_ref, g_ref, w_ref, qg_ref, kvg_ref, tab_ref, wuk_ref, wuv_ref,
                   cq_ref, k_ref, v_ref, sg_ref):
    x = x_ref[...]
    h = (x * g_ref[...]).astype(BF16)
    r = lax.rsqrt(jnp.mean(x * x, axis=-1, keepdims=True) + NORM_EPS)
    o1 = Q_LORA
    o2 = o1 + KV_LORA
    o3 = o2 + LANES
    q_scale = (MLA_NOPE + MLA_ROPE) ** -0.5 * math.log2(math.e)
    cq_ref[...] = _rms(_dot(h, w_ref[:, :o1]) * r, qg_ref[...] * q_scale).astype(BF16)
    c_kv = _rms(_dot(h, w_ref[:, o1:o2]) * r, kvg_ref[...]).astype(BF16)
    t = _dot(h, w_ref[:, o2:o3]) * (tab_ref[...] * r)
    t = t + pltpu.roll(t, MLA_ROPE, 1)
    lane = lax.broadcasted_iota(jnp.int32, t.shape, 1)
    k_tail = jnp.where(lane < MLA_ROPE, t, 0.0).astype(BF16)
    k_nope = _dot(c_kv, wuk_ref[...]).astype(BF16)
    for hd in range(MLA_HEADS):
        c0 = hd * MLA_QK_PAD
        k_ref[:, c0:c0 + MLA_NOPE] = k_nope[:, hd * MLA_NOPE:(hd + 1) * MLA_NOPE]
        k_ref[:, c0 + MLA_NOPE:c0 + MLA_QK_PAD] = k_tail
    vt = lax.dot_general(wuv_ref[...], c_kv, _NT, preferred_element_type=F32).astype(BF16)
    ones = jnp.ones((MLA_V_AUG - MLA_V, vt.shape[1]), BF16)
    for hd in range(MLA_HEADS):
        v_ref[0, hd, :MLA_V, :] = vt[hd * MLA_V:(hd + 1) * MLA_V]
        v_ref[0, hd, MLA_V:, :] = ones
    chunk = 512
    for c in range(MLA_WIDTH // chunk):
        gate = _dot(h, w_ref[:, o3 + c * chunk:o3 + (c + 1) * chunk]) * r
        sg_ref[:, c * chunk:(c + 1) * chunk] = _silu(gate).astype(BF16)


def _mla_in(x, g, w, qg, kvg, tab, wuk, wuv, tm):
    T = x.shape[0]
    kw = MLA_HEADS * MLA_QK_PAD
    return pl.pallas_call(
        _mla_in_kernel,
        grid=(T // tm,),
        in_specs=[_row_spec(tm, D_MODEL), _full_spec((1, D_MODEL)), _full_spec(w.shape),
                  _full_spec((1, Q_LORA)), _full_spec((1, KV_LORA)), _row_spec(tm, LANES),
                  _full_spec(wuk.shape), _full_spec(wuv.shape)],
        out_specs=[_row_spec(tm, Q_LORA), _row_spec(tm, kw),
                   pl.BlockSpec((1, MLA_HEADS, MLA_V_AUG, tm), lambda i: (i, 0, 0, 0)),
                   _row_spec(tm, MLA_WIDTH)],
        out_shape=[jax.ShapeDtypeStruct((T, Q_LORA), BF16), jax.ShapeDtypeStruct((T, kw), BF16),
                   jax.ShapeDtypeStruct((T // tm, MLA_HEADS, MLA_V_AUG, tm), BF16),
                   jax.ShapeDtypeStruct((T, MLA_WIDTH), BF16)],
        compiler_params=_params("parallel"),
        name="mla_in",
    )(x, g, w, qg, kvg, tab, wuk, wuv)


def _mla_q_kernel(cq_ref, w_ref, tabt_ref, q_ref):
    cq = cq_ref[...]
    half = MLA_ROPE // 2
    cos = tabt_ref[:half, :]
    sin = tabt_ref[MLA_ROPE:MLA_ROPE + half, :]
    group = 4
    per_head = MLA_NOPE + MLA_ROPE
    rows = group * per_head
    for g in range(MLA_HEADS // group):
        z = lax.dot_general(w_ref[g * rows:(g + 1) * rows, :], cq, _NT,
                            preferred_element_type=F32)
        for r in range(group):
            hd = g * group + r
            r0 = r * per_head
            q_ref[0, hd, :MLA_NOPE, :] = z[r0:r0 + MLA_NOPE].astype(BF16)
            x1 = z[r0 + MLA_NOPE:r0 + MLA_NOPE + half]
            x2 = z[r0 + MLA_NOPE + half:r0 + per_head]
            q_ref[0, hd, MLA_NOPE:MLA_NOPE + half, :] = (x1 * cos - x2 * sin).astype(BF16)
            q_ref[0, hd, MLA_NOPE + half:per_head, :] = (x2 * cos + x1 * sin).astype(BF16)
            q_ref[0, hd, per_head:, :] = jnp.zeros((MLA_QK_PAD - per_head, z.shape[1]), BF16)


def _mla_q(cq, wt, tabt, tq):
    T = cq.shape[0]
    return pl.pallas_call(
        _mla_q_kernel,
        grid=(T // tq,),
        in_specs=[_row_spec(tq, Q_LORA), _full_spec(wt.shape),
                  pl.BlockSpec((LANES, tq), lambda i: (0, i))],
        out_specs=pl.BlockSpec((1, MLA_HEADS, MLA_QK_PAD, tq), lambda i: (i, 0, 0, 0)),
        out_shape=jax.ShapeDtypeStruct((T // tq, MLA_HEADS, MLA_QK_PAD, tq), BF16),
        compiler_params=_params("parallel"),
        name="mla_q",
    )(cq, wt, tabt)


def _attn_kernel(qt_ref, k_ref, vt_ref, sg_ref, o_ref, acc_ref, s0_ref, s1_ref,
                 p0_ref, p1_ref, *, tq, hp, qb):
    tk = tq // 2

    def one_block(i, slot):
        rows = slice(slot * tq, (slot + 1) * tq)

        def stage_a(a, key0, s_ref):
            k = k_ref[0, pl.ds(pl.multiple_of(key0, tk), tk), a * MLA_QK_PAD:(a + 1) * MLA_QK_PAD]
            s = _dot(k, qt_ref[slot, a])
            s_ref[a] = s
            return jnp.max(s, axis=0, keepdims=True)

        def stage_b(a, s_ref, p_ref, m, cm):
            m_new = jnp.maximum(m, cm)
            p_ref[a] = jnp.exp2((s_ref[a] - m_new).astype(BF16))
            return m_new, jnp.exp2(m - m_new)

        def stage_c(a, p_ref, vt, alpha):
            acc_ref[slot, a] = alpha * acc_ref[slot, a] + _dot(vt, p_ref[a])

        tri = (lax.broadcasted_iota(jnp.int32, (tk, tk), 0)
               <= lax.broadcasted_iota(jnp.int32, (tk, tk), 1))
        state = []
        for a in range(hp):
            k_hi = k_ref[0, pl.ds(pl.multiple_of(i * tq + tk, tk), tk),
                         a * MLA_QK_PAD:(a + 1) * MLA_QK_PAD]
            s_hi = jnp.where(tri, _dot(k_hi, qt_ref[slot, a, :, tk:]), -1e30)
            cm_hi = jnp.max(s_hi, axis=0, keepdims=True)
            p0_ref[a, :, :tk] = jnp.zeros((tk, tk), BF16)
            p0_ref[a, :, tk:] = jnp.exp2((s_hi - cm_hi).astype(BF16))
            m = jnp.concatenate([jnp.full((1, tk), -jnp.inf, F32), cm_hi], axis=1)
            k_lo = k_ref[0, pl.ds(pl.multiple_of(i * tq, tk), tk),
                         a * MLA_QK_PAD:(a + 1) * MLA_QK_PAD]
            s_lo = _dot(k_lo, qt_ref[slot, a])
            s_tri = jnp.where(tri, s_lo[:, :tk], -1e30)
            s1_ref[a, :, :tk] = s_tri
            s1_ref[a, :, tk:] = s_lo[:, tk:]
            cm1 = jnp.concatenate([jnp.max(s_tri, axis=0, keepdims=True),
                                   jnp.max(s_lo[:, tk:], axis=0, keepdims=True)], axis=1)
            state.append((m, cm1, jnp.zeros((1, tq), F32)))
        acc_ref[slot] = jnp.zeros(acc_ref.shape[1:], F32)

        def pair(jj, state):
            blk = jnp.where(jj == 0, 2 * i + 1, 2 * (jj - 1))
            blk_next = jnp.where(jj == 0, 2 * i, 2 * (jj - 1) + 1)
            heads = range(hp)
            ms = [st[0] for st in state]
            cm1s = [st[1] for st in state]
            alpha0s = [st[2] for st in state]
            alpha1s = [None] * hp
            cm0s = [stage_a(a, jj * tq, s0_ref) for a in heads]
            for a in heads:
                stage_c(a, p0_ref, vt_ref[blk, a], alpha0s[a])
            for a in heads:
                ms[a], alpha1s[a] = stage_b(a, s1_ref, p1_ref, ms[a], cm1s[a])
            cm1s = [stage_a(a, jj * tq + tk, s1_ref) for a in heads]
            for a in heads:
                stage_c(a, p1_ref, vt_ref[blk_next, a], alpha1s[a])
            for a in heads:
                ms[a], alpha0s[a] = stage_b(a, s0_ref, p0_ref, ms[a], cm0s[a])
            return tuple(zip(ms, cm1s, alpha0s))

        state = lax.fori_loop(0, i, pair, tuple(state))
        blk = jnp.where(i == 0, 1, 2 * (i - 1))
        blk_next = jnp.where(i == 0, 0, 2 * (i - 1) + 1)
        for a in range(hp):
            m, cm1, alpha0 = state[a]
            _, alpha1 = stage_b(a, s1_ref, p1_ref, m, cm1)
            stage_c(a, p0_ref, vt_ref[blk, a], alpha0)
            stage_c(a, p1_ref, vt_ref[blk_next, a], alpha1)
            acc = acc_ref[slot, a]
            o = (acc[:MLA_V] / acc[MLA_V:MLA_V + 1]).T
            cs = slice(a * MLA_V, (a + 1) * MLA_V)
            o_ref[0, rows, cs] = (o * sg_ref[0, rows, cs].astype(F32)).astype(BF16)

    for slot in range(qb):
        one_block(pl.program_id(2) * qb + slot, slot)


def _attn(qt, k, vt, sg, tq, hp, qb):
    B, S, _ = k.shape
    ng = S // (qb * tq)
    tk = tq // 2
    return pl.pallas_call(
        functools.partial(_attn_kernel, tq=tq, hp=hp, qb=qb),
        grid=(B, MLA_HEADS // hp, ng),
        in_specs=[pl.BlockSpec((qb, hp, MLA_QK_PAD, tq), lambda b, h, i: (b * ng + i, h, 0, 0)),
                  pl.BlockSpec((1, S, hp * MLA_QK_PAD), lambda b, h, i: (b, 0, h)),
                  pl.BlockSpec((S // tk, hp, MLA_V_AUG, tk), lambda b, h, i: (b, h, 0, 0)),
                  pl.BlockSpec((1, qb * tq, hp * MLA_V), lambda b, h, i: (b, i, h))],
        out_specs=pl.BlockSpec((1, qb * tq, hp * MLA_V), lambda b, h, i: (b, i, h)),
        out_shape=jax.ShapeDtypeStruct((B, S, MLA_WIDTH), BF16),
        scratch_shapes=[pltpu.VMEM((qb, hp, MLA_V_AUG, tq), F32),
                        pltpu.VMEM((hp, tk, tq), F32), pltpu.VMEM((hp, tk, tq), F32),
                        pltpu.VMEM((hp, tk, tq), BF16), pltpu.VMEM((hp, tk, tq), BF16)],
        compiler_params=_params("parallel", "parallel", "arbitrary"),
        name="mla_attn",
    )(qt, k, vt, sg)


def _out_kernel(og_ref, x_ref, p_ref, wo_ref, pg_ref, wp_ref, wg_ref, out_ref):
    y = _dot(og_ref[...], wo_ref[...])
    x1 = x_ref[...] + _rms(y, pg_ref[...])
    pe = _dot(p_ref[...].astype(BF16), wp_ref[...])
    gt = jax.nn.sigmoid(_dot(x1.astype(BF16), wg_ref[...]))
    out_ref[...] = x1 + pe * gt


def _out(og, x, p, layer, wo, pg, wp, wg, tm):
    T = x.shape[0]
    width = og.shape[1]
    return pl.pallas_call(
        _out_kernel,
        grid=(T // tm,),
        in_specs=[_row_spec(tm, width), _row_spec(tm, D_MODEL),
                  pl.BlockSpec((None, tm, PLE_DIM), lambda i: (layer, i, 0)),
                  _full_spec(wo.shape), _full_spec((1, D_MODEL)),
                  _full_spec(wp.shape), _full_spec(wg.shape)],
        out_specs=_row_spec(tm, D_MODEL),
        out_shape=jax.ShapeDtypeStruct((T, D_MODEL), F32),
        compiler_params=_params("parallel"),
        name="branch_out",
    )(og, x, p, wo, pg, wp, wg)


def _rotate_half_cols(w):
    half = w.shape[-1] // 2
    return jnp.concatenate([-w[..., half:], w[..., :half]], axis=-1)


def _prep_mla_w_in(w):
    o2 = Q_LORA + KV_LORA
    o3 = o2 + MLA_ROPE
    kr = w[:, o2:o3]
    return jnp.concatenate([w[:, :o3], _rotate_half_cols(kr), w[:, o3:]], axis=1).astype(BF16)


def kernel(x, p, positions, pre_norm_g, post_norm_g, ret_w_in, ret_gn_g, ret_w_out, mla_w_in,
           mla_q_norm_g, mla_kv_norm_g, mla_w_uq, mla_w_uk, mla_w_uv, mla_w_out,
           ple_w_proj, ple_w_gate):
    B, S, D = x.shape
    T = B * S
    tm = min(512, T)
    tm_wide = 2 * tm
    tq = 2 * tm
    assert S % tq == 0
    n_chunks = min(2, S // RET_CHUNK)

    cos, sin, tab, tabt = _rope_tables(positions)
    xf = x.reshape(T, D)
    pf = p.reshape(DEPTH, T, PLE_DIM)

    for i in range(DEPTH):
        j = i // 2
        g_pre = pre_norm_g[i].reshape(1, D)
        if i % 2 == 0:
            q, k, v, sg = _ret_in(xf, g_pre, ret_w_in[j].astype(BF16), cos, sin, tm_wide)
            og = _ret_mix(q.reshape(B, S, -1), k.reshape(B, S, -1), v.reshape(B, S, -1),
                          sg.reshape(B, S, -1), ret_gn_g[j].reshape(1, RET_WIDTH), n_chunks)
            w_out = ret_w_out[j]
        else:
            wuk = mla_w_uk[j].reshape(KV_LORA, MLA_HEADS * MLA_NOPE).astype(BF16)
            wuv_t = mla_w_uv[j].reshape(KV_LORA, MLA_WIDTH).T.astype(BF16)
            cq, k, vt, sg = _mla_in(xf, g_pre, _prep_mla_w_in(mla_w_in[j]),
                                    mla_q_norm_g[j].reshape(1, Q_LORA),
                                    mla_kv_norm_g[j].reshape(1, KV_LORA), tab, wuk, wuv_t, tm)
            qt = _mla_q(cq, mla_w_uq[j].T.astype(BF16), tabt, tq)
            og = _attn(qt, k.reshape(B, S, -1), vt, sg.reshape(B, S, -1), tq, ATTN_HEADS_PER_STEP,
                       ATTN_QBLOCKS_PER_STEP)
            w_out = mla_w_out[j]
        xf = _out(og.reshape(T, -1), xf, pf, i, w_out.astype(BF16),
                  post_norm_g[i].reshape(1, D), ple_w_proj[i].astype(BF16),
                  ple_w_gate[i].astype(BF16), tm_wide)
    return xf.reshape(B, S, D)
```
